```python
import math
import jax, jax.numpy as jnp
from jax import lax
import numpy as np

D_MODEL = 2048
BATCH = 2
SEQ = 8192
DEPTH = 1

HEAD_DIM = 64
N_Q_HEADS = 16
N_KV_HEADS = 2
Q_PER_KV = N_Q_HEADS // N_KV_HEADS
ATTN_WIDTH = N_Q_HEADS * HEAD_DIM
KV_WIDTH = N_KV_HEADS * HEAD_DIM
WINDOW = 128
BLOCK = 128
ROPE_THETA = 10000.0
CONV_WIDTH = D_MODEL - ATTN_WIDTH
CONV_GROUPS = CONV_WIDTH // HEAD_DIM
SHORT_CONV_K = 3
MIX_WIDTH = ATTN_WIDTH + CONV_WIDTH
IN_WIDTH = ATTN_WIDTH + 2 * KV_WIDTH + 3 * CONV_WIDTH
D_FF = 5632
FFN_CONV_K = 3
LN_EPS = 1e-5
DEEPNORM_ALPHA = (2 * DEPTH) ** 0.25
DEEPNORM_BETA = (8 * DEPTH) ** -0.25
NEG_INF = -1e30

kernel_name = "hymba_swa_sink_shortconv_convffn_deepnorm"


def layer_norm(x, g, b):
    xf = x.astype(jnp.float32)
    mu = jnp.mean(xf, axis=-1, keepdims=True)
    var = jnp.mean(jnp.square(xf - mu), axis=-1, keepdims=True)
    y = (xf - mu) * lax.rsqrt(var + LN_EPS) * g.astype(jnp.float32) + b.astype(jnp.float32)
    return y.astype(x.dtype)


def causal_depthwise_conv(x, w):
    k = w.shape[0]
    return lax.conv_general_dilated(
        x, w[:, None, :].astype(x.dtype), window_strides=(1,), padding=[(k - 1, 0)],
        dimension_numbers=("NWC", "WIO", "NWC"), feature_group_count=x.shape[-1])


def rope(x, positions):
    half = HEAD_DIM // 2
    inv_freq = ROPE_THETA ** (-jnp.arange(half, dtype=jnp.float32) / half)
    ang = positions.astype(jnp.float32)[:, None] * inv_freq[None, :]
    cos = jnp.cos(ang)[None, :, None, :]
    sin = jnp.sin(ang)[None, :, None, :]
    xf = x.astype(jnp.float32)
    x1, x2 = xf[..., :half], xf[..., half:]
    out = jnp.concatenate([x1 * cos - x2 * sin, x2 * cos + x1 * sin], axis=-1)
    return out.astype(x.dtype)


def sliding_window_gqa(q, k, v, sinks):
    b, s = q.shape[0], q.shape[1]
    n = s // BLOCK
    qb = q.reshape(b, n, BLOCK, N_KV_HEADS, Q_PER_KV, HEAD_DIM)

    def band(t):
        tb = t.reshape(b, n, BLOCK, N_KV_HEADS, HEAD_DIM)
        prev = jnp.pad(tb, ((0, 0), (1, 0), (0, 0), (0, 0), (0, 0)))[:, :-1]
        return jnp.concatenate([prev, tb], axis=2)

    kb, vb = band(k), band(v)
    scores = jnp.einsum("bnqkgd,bnskd->bnkgqs", qb.astype(jnp.float32),
                        kb.astype(jnp.float32)) * (HEAD_DIM ** -0.5)
    qi = jnp.arange(BLOCK)[:, None]
    kj = jnp.arange(2 * BLOCK)[None, :]
    diff = BLOCK + qi - kj
    kpos = (jnp.arange(n)[:, None, None] - 1) * BLOCK + kj[None]
    valid = (diff >= 0) & (diff < WINDOW) & (kpos >= 0)
    scores = jnp.where(valid[None, :, None, None], scores, NEG_INF)
    sink = jnp.broadcast_to(
        sinks.astype(jnp.float32).reshape(1, 1, N_KV_HEADS, Q_PER_KV, 1, 1),
        scores.shape[:-1] + (1,))
    probs = jax.nn.softmax(jnp.concatenate([scores, sink], axis=-1), axis=-1)[..., :-1]
    out = jnp.einsum("bnkgqs,bnskd->bnqkgd", probs.astype(v.dtype), vb)
    return out.reshape(b, s, ATTN_WIDTH)


def setup_inputs(seed: int = 0) -> dict:
    key = jax.random.key(seed)
    ks = jax.random.split(key, 12)
    f32 = jnp.float32
    x = jax.random.normal(ks[0], (BATCH, SEQ, D_MODEL), f32)
    w_in = jax.random.normal(ks[1], (DEPTH, D_MODEL, IN_WIDTH), f32) * D_MODEL ** -0.5
    attn_sinks = jax.random.normal(ks[2], (DEPTH, N_Q_HEADS), f32) * 0.5
    short_conv_w = jax.random.normal(ks[3], (DEPTH, SHORT_CONV_K, CONV_WIDTH), f32) * SHORT_CONV_K ** -0.5
    w_out = jax.random.normal(ks[4], (DEPTH, MIX_WIDTH, D_MODEL), f32) * (MIX_WIDTH ** -0.5 * DEEPNORM_BETA)
    ln1_g = 1.0 + 0.02 * jax.random.normal(ks[5], (DEPTH, D_MODEL), f32)
    ln1_b = 0.02 * jax.random.normal(ks[6], (DEPTH, D_MODEL), f32)
    ffn_w_up = jax.random.normal(ks[7], (DEPTH, D_MODEL, 2 * D_FF), f32) * D_MODEL ** -0.5
    ffn_conv_w = jax.random.normal(ks[8], (DEPTH, FFN_CONV_K, 2 * D_FF), f32) * FFN_CONV_K ** -0.5
    ffn_w_down = jax.random.normal(ks[9], (DEPTH, D_FF, D_MODEL), f32) * (D_FF ** -0.5 * DEEPNORM_BETA)
    ln2_g = 1.0 + 0.02 * jax.random.normal(ks[10], (DEPTH, D_MODEL), f32)
    ln2_b = 0.02 * jax.random.normal(ks[11], (DEPTH, D_MODEL), f32)
    return {"x": x, "w_in": w_in, "attn_sinks": attn_sinks, "short_conv_w": short_conv_w,
            "w_out": w_out, "ln1_g": ln1_g, "ln1_b": ln1_b, "ffn_w_up": ffn_w_up,
            "ffn_conv_w": ffn_conv_w, "ffn_w_down": ffn_w_down, "ln2_g": ln2_g, "ln2_b": ln2_b}


def reference(x, w_in, attn_sinks, short_conv_w, w_out, ln1_g, ln1_b,
              ffn_w_up, ffn_conv_w, ffn_w_down, ln2_g, ln2_b):
    b, s = x.shape[0], x.shape[1]
    positions = jnp.arange(s)
    split_pts = [ATTN_WIDTH,
                 ATTN_WIDTH + KV_WIDTH,
                 ATTN_WIDTH + 2 * KV_WIDTH,
                 ATTN_WIDTH + 2 * KV_WIDTH + CONV_WIDTH,
                 ATTN_WIDTH + 2 * KV_WIDTH + 2 * CONV_WIDTH]
    for l in range(DEPTH):
        proj = jnp.einsum("bsd,de->bse", x, w_in[l])
        q, k, v, gate_b, gate_c, h = jnp.split(proj, split_pts, axis=-1)
        q = rope(q.reshape(b, s, N_Q_HEADS, HEAD_DIM), positions)
        k = rope(k.reshape(b, s, N_KV_HEADS, HEAD_DIM), positions)
        v = v.reshape(b, s, N_KV_HEADS, HEAD_DIM)
        attn = sliding_window_gqa(q, k, v, attn_sinks[l])
        conv = gate_b * causal_depthwise_conv(gate_c * h, short_conv_w[l])
        mix = jnp.concatenate([attn, conv], axis=-1)
        y = jnp.einsum("bsm,md->bsd", mix, w_out[l])
        x = layer_norm(DEEPNORM_ALPHA * x + y, ln1_g[l], ln1_b[l])
        u = jnp.einsum("bsd,df->bsf", x, ffn_w_up[l])
        u = causal_depthwise_conv(u, ffn_conv_w[l])
        a, g = jnp.split(u, 2, axis=-1)
        y = jnp.einsum("bsf,fd->bsd", jax.nn.silu(a) * g, ffn_w_down[l])
        x = layer_norm(DEEPNORM_ALPHA * x + y, ln2_g[l], ln2_b[l])
    return x
```

```python
import functools

import jax
import jax.numpy as jnp
from jax import lax
from jax.experimental import pallas as pl
from jax.experimental.pallas import tpu as pltpu

D_MODEL = 2048
HEAD_DIM = 64
N_Q_HEADS = 16
N_KV_HEADS = 2
Q_PER_KV = N_Q_HEADS // N_KV_HEADS
ATTN_WIDTH = N_Q_HEADS * HEAD_DIM
KV_WIDTH = N_KV_HEADS * HEAD_DIM
WINDOW = 128
BLOCK = 128
ROPE_THETA = 10000.0
CONV_WIDTH = D_MODEL - ATTN_WIDTH
CONV_K = 3
D_FF = 5632
LN_EPS = 1e-5
DEPTH = 1
DEEPNORM_ALPHA = (2 * DEPTH) ** 0.25
NEG_INF = -1e30

LANES = 128
SUBLANES = 8
VMEM_LIMIT_BYTES = 60000 * 1024

TM_IN = 512
TQ_ATTN = 512
TM_OUT = 512
TM_FFN = 1024
TF_FFN = 512

BF16 = jnp.bfloat16
F32 = jnp.float32


def _dot(a, b):
    return jnp.dot(a, b, preferred_element_type=F32)


def _shift_rows(u, halo, k):
    r = pltpu.roll(u, k, 0)
    hr = pltpu.roll(halo, k, 0)
    row = lax.broadcasted_iota(jnp.int32, hr.shape, 0)
    head = jnp.where(row < k, hr, r[:SUBLANES])
    return jnp.concatenate([head, r[SUBLANES:]], axis=0)


def _causal_conv3(u, halo, w_ref):
    return (w_ref[2:3, :] * u
            + w_ref[1:2, :] * _shift_rows(u, halo, 1)
            + w_ref[0:1, :] * _shift_rows(u, halo, 2))


def _layer_norm(z, g, b):
    mu = jnp.mean(z, axis=-1, keepdims=True)
    zc = z - mu
    var = jnp.mean(zc * zc, axis=-1, keepdims=True)
    return zc * lax.rsqrt(var + LN_EPS) * g + b


def _inproj_kernel(x_ref, wq_ref, wkv_ref, wbch_ref, cos_ref, sin_ref, cw_ref,
                   q_ref, kd_ref, vd_ref, conv_ref, carry_ref, *, tiles_per_seq):
    i = pl.program_id(0)
    tm = x_ref.shape[0]
    xb = x_ref[...].astype(BF16)
    cos = cos_ref[...]
    sin = sin_ref[...]
    lane = lax.broadcasted_iota(jnp.int32, (tm, LANES), 1)
    first_half = (lane & (HEAD_DIM // 2)) == 0
    low_head = lane < HEAD_DIM

    def rope(t):
        partner = jnp.where(first_half, pltpu.roll(t, LANES - HEAD_DIM // 2, 1),
                            pltpu.roll(t, HEAD_DIM // 2, 1))
        return t * cos + partner * sin

    def dup_heads(t):
        r = pltpu.roll(t, HEAD_DIM, 1)
        return jnp.where(low_head, t, r), jnp.where(low_head, r, t)

    q = _dot(xb, wq_ref[...])
    scale = HEAD_DIM ** -0.5
    for c in range(ATTN_WIDTH // LANES):
        sl = slice(c * LANES, (c + 1) * LANES)
        q_ref[:, sl] = (rope(q[:, sl]) * scale).astype(BF16)

    kv = _dot(xb, wkv_ref[...])
    k0, k1 = dup_heads(rope(kv[:, :KV_WIDTH]))
    v0, v1 = dup_heads(kv[:, KV_WIDTH:])
    kd_ref[:, :LANES] = k0.astype(BF16)
    kd_ref[:, LANES:] = k1.astype(BF16)
    vd_ref[:, :LANES] = v0.astype(BF16)
    vd_ref[:, LANES:] = v1.astype(BF16)

    bch = _dot(xb, wbch_ref[...])
    gate_b = bch[:, :CONV_WIDTH]
    u = bch[:, CONV_WIDTH:2 * CONV_WIDTH] * bch[:, 2 * CONV_WIDTH:]

    @pl.when(i % tiles_per_seq == 0)
    def _():
        carry_ref[...] = jnp.zeros_like(carry_ref)

    halo = carry_ref[...]
    carry_ref[...] = u[tm - SUBLANES:, :]
    conv_ref[...] = (gate_b * _causal_conv3(u, halo, cw_ref)).astype(BF16)


def _inproj(x2, wq, wkv, wbch, cos_t, sin_t, conv_w, seq):
    t = x2.shape[0]
    tm = TM_IN
    tiles_per_seq = seq // tm
    const = lambda i: (0, 0)
    resident = functools.partial(pl.BlockSpec, index_map=const, pipeline_mode=pl.Buffered(1))
    row = lambda i: (i, 0)
    return pl.pallas_call(
        functools.partial(_inproj_kernel, tiles_per_seq=tiles_per_seq),
        grid=(t // tm,),
        in_specs=[
            pl.BlockSpec((tm, D_MODEL), row),
            resident((D_MODEL, ATTN_WIDTH)),
            resident((D_MODEL, 2 * KV_WIDTH)),
            resident((D_MODEL, 3 * CONV_WIDTH)),
            pl.BlockSpec((tm, LANES), lambda i: (i % tiles_per_seq, 0)),
            pl.BlockSpec((tm, LANES), lambda i: (i % tiles_per_seq, 0)),
            pl.BlockSpec((CONV_K, CONV_WIDTH), const),
        ],
        out_specs=[
            pl.BlockSpec((tm, ATTN_WIDTH), row),
            pl.BlockSpec((tm, 2 * LANES), row),
            pl.BlockSpec((tm, 2 * LANES), row),
            pl.BlockSpec((tm, CONV_WIDTH), row),
        ],
        out_shape=[
            jax.ShapeDtypeStruct((t, ATTN_WIDTH), BF16),
            jax.ShapeDtypeStruct((t, 2 * LANES), BF16),
            jax.ShapeDtypeStruct((t, 2 * LANES), BF16),
            jax.ShapeDtypeStruct((t, CONV_WIDTH), BF16),
        ],
        scratch_shapes=[pltpu.VMEM((SUBLANES, CONV_WIDTH), F32)],
        compiler_params=pltpu.CompilerParams(
            dimension_semantics=("arbitrary",), vmem_limit_bytes=VMEM_LIMIT_BYTES),
        name="inproj",
    )(x2, wq, wkv, wbch, cos_t, sin_t, conv_w)


def _attn_kernel(sink_ref, q_ref, kc_ref, kp_ref, vc_ref, vp_ref, o_ref, *, blocks_per_seq):
    i = pl.program_id(0)
    tq = q_ref.shape[0]
    nb = tq // BLOCK
    kfull = jnp.concatenate([kp_ref[...], kc_ref[...]], axis=0)
    vfull = jnp.concatenate([vp_ref[...], vc_ref[...]], axis=0)
    qi = lax.broadcasted_iota(jnp.int32, (BLOCK, 2 * BLOCK), 0)
    kj = lax.broadcasted_iota(jnp.int32, (BLOCK, 2 * BLOCK), 1)
    in_window = (kj > qi) & (kj <= qi + WINDOW)
    lane = lax.broadcasted_iota(jnp.int32, (BLOCK, LANES), 1)
    low_head = lane < HEAD_DIM
    zero = jnp.zeros((BLOCK, LANES), BF16)

    for b in range(nb):
        rows = slice(b * BLOCK, (b + 1) * BLOCK)
        band = slice(b * BLOCK, (b + 2) * BLOCK)
        seq_start = (i * nb + b) % blocks_per_seq == 0
        valid = in_window & ((kj >= BLOCK) | jnp.logical_not(seq_start))
        for kvh in range(N_KV_HEADS):
            cols = slice(kvh * LANES, (kvh + 1) * LANES)
            k2 = kfull[band, cols]
            v2 = vfull[band, cols]
            qs = []
            for p in range(Q_PER_KV // 2):
                c0 = kvh * Q_PER_KV * HEAD_DIM + p * LANES
                qp = q_ref[rows, c0:c0 + LANES]
                qs.append(jnp.where(low_head, qp, zero))
                qs.append(jnp.where(low_head, zero, qp))
            s_all = lax.dot_general(jnp.concatenate(qs, axis=0), k2,
                                    (((1,), (1,)), ((), ())), preferred_element_type=F32)
            ps, inv_l = [], []
            for g in range(Q_PER_KV):
                s = jnp.where(valid, s_all[g * BLOCK:(g + 1) * BLOCK], NEG_INF)
                sink = sink_ref[kvh * Q_PER_KV + g]
                m = jnp.maximum(jnp.max(s, axis=-1, keepdims=True), sink)
                pr = jnp.exp(s - m)
                l = jnp.sum(pr, axis=-1, keepdims=True) + jnp.exp(sink - m)
                ps.append(pr.astype(BF16))
                inv_l.append(1.0 / l)
            o_all = _dot(jnp.concatenate(ps, axis=0), v2)
            for p in range(Q_PER_KV // 2):
                g0, g1 = 2 * p, 2 * p + 1
                o0 = o_all[g0 * BLOCK:(g0 + 1) * BLOCK] * inv_l[g0]
                o1 = o_all[g1 * BLOCK:(g1 + 1) * BLOCK] * inv_l[g1]
                c0 = kvh * Q_PER_KV * HEAD_DIM + p * LANES
                o_ref[rows, c0:c0 + LANES] = jnp.where(low_head, o0, o1).astype(BF16)


def _attn(sinks, q, kd, vd, seq):
    t = q.shape[0]
    tq = TQ_ATTN
    nb = tq // BLOCK
    row = lambda i: (i, 0)
    prev = lambda i: (jnp.maximum(i * nb - 1, 0), 0)
    return pl.pallas_call(
        functools.partial(_attn_kernel, blocks_per_seq=seq // BLOCK),
        grid=(t // tq,),
        in_specs=[
            pl.BlockSpec(memory_space=pltpu.SMEM),
            pl.BlockSpec((tq, ATTN_WIDTH), row),
            pl.BlockSpec((tq, 2 * LANES), row),
            pl.BlockSpec((BLOCK, 2 * LANES), prev),
            pl.BlockSpec((tq, 2 * LANES), row),
            pl.BlockSpec((BLOCK, 2 * LANES), prev),
        ],
        out_specs=pl.BlockSpec((tq, ATTN_WIDTH), row),
        out_shape=jax.ShapeDtypeStruct((t, ATTN_WIDTH), BF16),
        compiler_params=pltpu.CompilerParams(
            dimension_semantics=("arbitrary",), vmem_limit_bytes=VMEM_LIMIT_BYTES),
        name="attn",
    )(sinks, q, kd, kd, vd, vd)


def _outproj_kernel(x_ref, a_ref, c_ref, wa_ref, wc_ref, g_ref, b_ref, o_ref):
    y = _dot(a_ref[...], wa_ref[...]) + _dot(c_ref[...], wc_ref[...])
    z = DEEPNORM_ALPHA * x_ref[...] + y
    o_ref[...] = _layer_norm(z, g_ref[...], b_ref[...]).astype(o_ref.dtype)


def _outproj(x2, attn, conv, w_out, g, b):
    t = x2.shape[0]
    tm = TM_OUT
    row = lambda i: (i, 0)
    const = lambda i: (0, 0)
    return pl.pallas_call(
        _outproj_kernel,
        grid=(t // tm,),
        in_specs=[
            pl.BlockSpec((tm, D_MODEL), row),
            pl.BlockSpec((tm, ATTN_WIDTH), row),
            pl.BlockSpec((tm, CONV_WIDTH), row),
            pl.BlockSpec((ATTN_WIDTH, D_MODEL), const),
            pl.BlockSpec((CONV_WIDTH, D_MODEL), lambda i: (1, 0)),
            pl.BlockSpec((1, D_MODEL), const),
            pl.BlockSpec((1, D_MODEL), const),
        ],
        out_specs=pl.BlockSpec((tm, D_MODEL), row),
        out_shape=jax.ShapeDtypeStruct((t, D_MODEL), BF16),
        compiler_params=pltpu.CompilerParams(
            dimension_semantics=("arbitrary",), vmem_limit_bytes=VMEM_LIMIT_BYTES),
        name="outproj",
    )(x2, attn, conv, w_out, w_out, g, b)


def _ffn_kernel(x_ref, wa_ref, wg_ref, cwa_ref, cwg_ref, wd_ref, g_ref, b_ref,
                o_ref, carry_a, carry_g, *, tiles_per_seq):
    i = pl.program_id(0)
    j = pl.program_id(1)
    tm = x_ref.shape[0]
    x = x_ref[...]
    ua = _dot(x, wa_ref[...])
    ug = _dot(x, wg_ref[...])

    @pl.when(i % tiles_per_seq == 0)
    def _():
        carry_a[j] = jnp.zeros(carry_a.shape[1:], F32)
        carry_g[j] = jnp.zeros(carry_g.shape[1:], F32)

    halo_a = carry_a[j]
    halo_g = carry_g[j]
    carry_a[j] = ua[tm - SUBLANES:, :]
    carry_g[j] = ug[tm - SUBLANES:, :]
    a = _causal_conv3(ua, halo_a, cwa_ref)
    gate = _causal_conv3(ug, halo_g, cwg_ref)
    h = (a * jax.nn.sigmoid(a) * gate).astype(BF16)
    y = _dot(h, wd_ref[...])

    @pl.when(j == 0)
    def _():
        o_ref[...] = jnp.zeros_like(o_ref)

    o_ref[...] += y

    @pl.when(j == pl.num_programs(1) - 1)
    def _():
        z = DEEPNORM_ALPHA * x.astype(F32) + o_ref[...]
        o_ref[...] = _layer_norm(z, g_ref[...], b_ref[...])


def _ffn(x1, w_up, conv_w, w_down, g, b, seq):
    t = x1.shape[0]
    tm, tf = TM_FFN, TF_FFN
    nf = D_FF // tf
    const = lambda i, j: (0, 0)
    return pl.pallas_call(
        functools.partial(_ffn_kernel, tiles_per_seq=seq // tm),
        grid=(t // tm, nf),
        in_specs=[
            pl.BlockSpec((tm, D_MODEL), lambda i, j: (i, 0)),
            pl.BlockSpec((D_MODEL, tf), lambda i, j: (0, j)),
            pl.BlockSpec((D_MODEL, tf), lambda i, j: (0, j + nf)),
            pl.BlockSpec((CONV_K, tf), lambda i, j: (0, j)),
            pl.BlockSpec((CONV_K, tf), lambda i, j: (0, j + nf)),
            pl.BlockSpec((tf, D_MODEL), lambda i, j: (j, 0)),
            pl.BlockSpec((1, D_MODEL), const),
            pl.BlockSpec((1, D_MODEL), const),
        ],
        out_specs=pl.BlockSpec((tm, D_MODEL), lambda i, j: (i, 0)),
        out_shape=jax.ShapeDtypeStruct((t, D_MODEL), F32),
        scratch_shapes=[pltpu.VMEM((nf, SUBLANES, tf), F32),
                        pltpu.VMEM((nf, SUBLANES, tf), F32)],
        compiler_params=pltpu.CompilerParams(
            dimension_semantics=("arbitrary", "arbitrary"), vmem_limit_bytes=VMEM_LIMIT_BYTES),
        name="ffn",
    )(x1, w_up, w_up, conv_w, conv_w, w_down, g, b)


def _rope_tables(seq):
    half = HEAD_DIM // 2
    inv_freq = ROPE_THETA ** (-jnp.arange(half, dtype=F32) / half)
    ang = jnp.arange(seq).astype(F32)[:, None] * inv_freq[None, :]
    cos = jnp.cos(ang)
    sin = jnp.sin(ang)
    reps = LANES // HEAD_DIM
    return (jnp.tile(jnp.concatenate([cos, cos], axis=-1), (1, reps)),
            jnp.tile(jnp.concatenate([-sin, sin], axis=-1), (1, reps)))


def kernel(x, w_in, attn_sinks, short_conv_w, w_out, ln1_g, ln1_b,
           ffn_w_up, ffn_conv_w, ffn_w_down, ln2_g, ln2_b):
    b, s, d = x.shape
    assert d == D_MODEL and s % max(TM_IN, TQ_ATTN, TM_FFN) == 0
    cos_t, sin_t = _rope_tables(s)
    kv_end = ATTN_WIDTH + 2 * KV_WIDTH
    h = x.reshape(b * s, d)
    for l in range(w_in.shape[0]):
        wq = w_in[l, :, :ATTN_WIDTH].astype(BF16)
        wkv = w_in[l, :, ATTN_WIDTH:kv_end].astype(BF16)
        wbch = w_in[l, :, kv_end:].astype(BF16)
        q, kd, vd, conv = _inproj(h, wq, wkv, wbch, cos_t, sin_t, short_conv_w[l], s)
        attn = _attn(attn_sinks[l], q, kd, vd, s)
        x1 = _outproj(h, attn, conv, w_out[l].astype(BF16), ln1_g[l][None], ln1_b[l][None])
        h = _ffn(x1, ffn_w_up[l].astype(BF16), ffn_conv_w[l], ffn_w_down[l].astype(BF16),
                 ln2_g[l][None], ln2_b[l][None], s)
    return h.reshape(b, s, d)
```

```python
import functools

import jax
import jax.numpy as jnp
from jax import lax
from jax.experimental import pallas as pl
from jax.experimental.pallas import tpu as pltpu

D_MODEL = 2048
HEAD_DIM = 64
N_Q_HEADS = 16
N_KV_HEADS = 2
Q_PER_KV = N_Q_HEADS // N_KV_HEADS
ATTN_WIDTH = N_Q_HEADS * HEAD_DIM
KV_WIDTH = N_KV_HEADS * HEAD_DIM
WINDOW = 128
BLOCK = 128
ROPE_THETA = 10000.0
CONV_WIDTH = D_MODEL - ATTN_WIDTH
CONV_K = 3
D_FF = 5632
LN_EPS = 1e-5
DEPTH = 1
DEEPNORM_ALPHA = (2 * DEPTH) ** 0.25
NEG_INF = -1e30

LANES = 128
SUBLANES = 8
VMEM_LIMIT_BYTES = 60000 * 1024

TM_IN = 512
SB_IN = 256
TQ_ATTN = 512
TM_OUT = 1024
SB_OUT = 256
TM_FFN = 1024
TF_FFN = 512
SB_FFN = 512

BF16 = jnp.bfloat16
F32 = jnp.float32


def _dot(a, b):
    return jnp.dot(a, b, preferred_element_type=F32)


def _shift_rows(u, halo, k):
    r = pltpu.roll(u, k, 0)
    hr = pltpu.roll(halo, k, 0)
    row = lax.broadcasted_iota(jnp.int32, hr.shape, 0)
    head = jnp.where(row < k, hr, r[:SUBLANES])
    return jnp.concatenate([head, r[SUBLANES:]], axis=0)


def _causal_conv3(u, halo, w_ref):
    return (w_ref[2:3, :] * u
            + w_ref[1:2, :] * _shift_rows(u, halo, 1)
            + w_ref[0:1, :] * _shift_rows(u, halo, 2))


def _layer_norm(z, g, b):
    mu = jnp.mean(z, axis=-1, keepdims=True)
    zc = z - mu
    var = jnp.mean(zc * zc, axis=-1, keepdims=True)
    return zc * lax.rsqrt(var + LN_EPS) * g + b


def _inproj_kernel(x_ref, wq_ref, wkv_ref, wbch_ref, cos_ref, sin_ref, cw_ref,
                   q_ref, kd_ref, vd_ref, conv_ref, carry_ref, *, tiles_per_seq):
    i = pl.program_id(0)
    tm = x_ref.shape[0]
    sb = SB_IN
    nsub = tm // sb
    lane = lax.broadcasted_iota(jnp.int32, (sb, LANES), 1)
    first_half = (lane & (HEAD_DIM // 2)) == 0
    low_head = lane < HEAD_DIM
    scale = HEAD_DIM ** -0.5

    @pl.when(i % tiles_per_seq == 0)
    def _():
        carry_ref[...] = jnp.zeros_like(carry_ref)

    def dup_heads(t):
        r = pltpu.roll(t, HEAD_DIM, 1)
        return jnp.where(low_head, t, r), jnp.where(low_head, r, t)

    def project(r):
        xb = x_ref[r * sb:(r + 1) * sb, :].astype(BF16)
        return _dot(xb, wbch_ref[...]), _dot(xb, wq_ref[...]), _dot(xb, wkv_ref[...])

    halo = carry_ref[...]
    nxt = project(0)
    for r in range(nsub):
        rows = slice(r * sb, (r + 1) * sb)
        bch, q, kv = nxt
        if r + 1 < nsub:
            nxt = project(r + 1)
        cos = cos_ref[rows, :]
        sin = sin_ref[rows, :]

        def rope(t):
            partner = jnp.where(first_half, pltpu.roll(t, LANES - HEAD_DIM // 2, 1),
                                pltpu.roll(t, HEAD_DIM // 2, 1))
            return t * cos + partner * sin

        gate_b = bch[:, :CONV_WIDTH]
        u = bch[:, CONV_WIDTH:2 * CONV_WIDTH] * bch[:, 2 * CONV_WIDTH:]
        conv_ref[rows, :] = (gate_b * _causal_conv3(u, halo, cw_ref)).astype(BF16)
        halo = u[sb - SUBLANES:, :]
        for c in range(ATTN_WIDTH // LANES):
            sl = slice(c * LANES, (c + 1) * LANES)
            q_ref[rows, sl] = (rope(q[:, sl]) * scale).astype(BF16)
        k0, k1 = dup_heads(rope(kv[:, :KV_WIDTH]))
        v0, v1 = dup_heads(kv[:, KV_WIDTH:])
        kd_ref[rows, :LANES] = k0.astype(BF16)
        kd_ref[rows, LANES:] = k1.astype(BF16)
        vd_ref[rows, :LANES] = v0.astype(BF16)
        vd_ref[rows, LANES:] = v1.astype(BF16)
    carry_ref[...] = halo


def _inproj(x2, wq, wkv, wbch, cos_t, sin_t, conv_w, seq):
    t = x2.shape[0]
    tm = TM_IN
    tiles_per_seq = seq // tm
    const = lambda i: (0, 0)
    resident = functools.partial(pl.BlockSpec, index_map=const, pipeline_mode=pl.Buffered(1))
    row = lambda i: (i, 0)
    return pl.pallas_call(
        functools.partial(_inproj_kernel, tiles_per_seq=tiles_per_seq),
        grid=(t // tm,),
        in_specs=[
            pl.BlockSpec((tm, D_MODEL), row),
            resident((D_MODEL, ATTN_WIDTH)),
            resident((D_MODEL, 2 * KV_WIDTH)),
            resident((D_MODEL, 3 * CONV_WIDTH)),
            pl.BlockSpec((tm, LANES), lambda i: (i % tiles_per_seq, 0)),
            pl.BlockSpec((tm, LANES), lambda i: (i % tiles_per_seq, 0)),
            pl.BlockSpec((CONV_K, CONV_WIDTH), const),
        ],
        out_specs=[
            pl.BlockSpec((tm, ATTN_WIDTH), row),
            pl.BlockSpec((tm, 2 * LANES), row),
            pl.BlockSpec((tm, 2 * LANES), row),
            pl.BlockSpec((tm, CONV_WIDTH), row),
        ],
        out_shape=[
            jax.ShapeDtypeStruct((t, ATTN_WIDTH), BF16),
            jax.ShapeDtypeStruct((t, 2 * LANES), BF16),
            jax.ShapeDtypeStruct((t, 2 * LANES), BF16),
            jax.ShapeDtypeStruct((t, CONV_WIDTH), BF16),
        ],
        scratch_shapes=[pltpu.VMEM((SUBLANES, CONV_WIDTH), F32)],
        compiler_params=pltpu.CompilerParams(
            dimension_semantics=("arbitrary",), vmem_limit_bytes=VMEM_LIMIT_BYTES),
        name="inproj",
    )(x2, wq, wkv, wbch, cos_t, sin_t, conv_w)


def _attn_kernel(sink_ref, q_ref, kc_ref, kp_ref, vc_ref, vp_ref, o_ref, *, blocks_per_seq):
    i = pl.program_id(0)
    tq = q_ref.shape[0]
    nb = tq // BLOCK
    kfull = jnp.concatenate([kp_ref[...], kc_ref[...]], axis=0)
    vfull = jnp.concatenate([vp_ref[...], vc_ref[...]], axis=0)
    qi = lax.broadcasted_iota(jnp.int32, (BLOCK, 2 * BLOCK), 0)
    kj = lax.broadcasted_iota(jnp.int32, (BLOCK, 2 * BLOCK), 1)
    in_window = (kj > qi) & (kj <= qi + WINDOW)
    lane = lax.broadcasted_iota(jnp.int32, (BLOCK, LANES), 1)
    low_head = lane < HEAD_DIM
    zero = jnp.zeros((BLOCK, LANES), BF16)

    for b in range(nb):
        rows = slice(b * BLOCK, (b + 1) * BLOCK)
        band = slice(b * BLOCK, (b + 2) * BLOCK)
        seq_start = (i * nb + b) % blocks_per_seq == 0
        valid = in_window & ((kj >= BLOCK) | jnp.logical_not(seq_start))
        for kvh in range(N_KV_HEADS):
            cols = slice(kvh * LANES, (kvh + 1) * LANES)
            k2 = kfull[band, cols]
            v2 = vfull[band, cols]
            qs = []
            for p in range(Q_PER_KV // 2):
                c0 = kvh * Q_PER_KV * HEAD_DIM + p * LANES
                qp = q_ref[rows, c0:c0 + LANES]
                qs.append(jnp.where(low_head, qp, zero))
                qs.append(jnp.where(low_head, zero, qp))
            s_all = lax.dot_general(jnp.concatenate(qs, axis=0), k2,
                                    (((1,), (1,)), ((), ())), preferred_element_type=F32)
            ps, inv_l = [], []
            for g in range(Q_PER_KV):
                s = jnp.where(valid, s_all[g * BLOCK:(g + 1) * BLOCK], NEG_INF)
                sink = sink_ref[kvh * Q_PER_KV + g]
                m = jnp.maximum(jnp.max(s, axis=-1, keepdims=True), sink)
                pr = jnp.exp(s - m)
                l = jnp.sum(pr, axis=-1, keepdims=True) + jnp.exp(sink - m)
                ps.append(pr.astype(BF16))
                inv_l.append(1.0 / l)
            o_all = _dot(jnp.concatenate(ps, axis=0), v2)
            for p in range(Q_PER_KV // 2):
                g0, g1 = 2 * p, 2 * p + 1
                o0 = o_all[g0 * BLOCK:(g0 + 1) * BLOCK] * inv_l[g0]
                o1 = o_all[g1 * BLOCK:(g1 + 1) * BLOCK] * inv_l[g1]
                c0 = kvh * Q_PER_KV * HEAD_DIM + p * LANES
                o_ref[rows, c0:c0 + LANES] = jnp.where(low_head, o0, o1).astype(BF16)


def _attn(sinks, q, kd, vd, seq):
    t = q.shape[0]
    tq = TQ_ATTN
    nb = tq // BLOCK
    row = lambda i: (i, 0)
    prev = lambda i: (jnp.maximum(i * nb - 1, 0), 0)
    return pl.pallas_call(
        functools.partial(_attn_kernel, blocks_per_seq=seq // BLOCK),
        grid=(t // tq,),
        in_specs=[
            pl.BlockSpec(memory_space=pltpu.SMEM),
            pl.BlockSpec((tq, ATTN_WIDTH), row),
            pl.BlockSpec((tq, 2 * LANES), row),
            pl.BlockSpec((BLOCK, 2 * LANES), prev),
            pl.BlockSpec((tq, 2 * LANES), row),
            pl.BlockSpec((BLOCK, 2 * LANES), prev),
        ],
        out_specs=pl.BlockSpec((tq, ATTN_WIDTH), row),
        out_shape=jax.ShapeDtypeStruct((t, ATTN_WIDTH), BF16),
        compiler_params=pltpu.CompilerParams(
            dimension_semantics=("arbitrary",), vmem_limit_bytes=VMEM_LIMIT_BYTES),
        name="attn",
    )(sinks, q, kd, kd, vd, vd)


def _outproj_kernel(x_ref, a_ref, c_ref, wa_ref, wc_ref, g_ref, b_ref, o_ref):
    sb = SB_OUT
    nsub = x_ref.shape[0] // sb

    def project(r):
        rows = slice(r * sb, (r + 1) * sb)
        return _dot(a_ref[rows, :], wa_ref[...]) + _dot(c_ref[rows, :], wc_ref[...])

    nxt = project(0)
    for r in range(nsub):
        rows = slice(r * sb, (r + 1) * sb)
        y = nxt
        if r + 1 < nsub:
            nxt = project(r + 1)
        z = DEEPNORM_ALPHA * x_ref[rows, :] + y
        o_ref[rows, :] = _layer_norm(z, g_ref[...], b_ref[...]).astype(o_ref.dtype)


def _outproj(x2, attn, conv, w_out, g, b):
    t = x2.shape[0]
    tm = TM_OUT
    row = lambda i: (i, 0)
    const = lambda i: (0, 0)
    return pl.pallas_call(
        _outproj_kernel,
        grid=(t // tm,),
        in_specs=[
            pl.BlockSpec((tm, D_MODEL), row),
            pl.BlockSpec((tm, ATTN_WIDTH), row),
            pl.BlockSpec((tm, CONV_WIDTH), row),
            pl.BlockSpec((ATTN_WIDTH, D_MODEL), const, pipeline_mode=pl.Buffered(1)),
            pl.BlockSpec((CONV_WIDTH, D_MODEL), lambda i: (1, 0), pipeline_mode=pl.Buffered(1)),
            pl.BlockSpec((1, D_MODEL), const),
            pl.BlockSpec((1, D_MODEL), const),
        ],
        out_specs=pl.BlockSpec((tm, D_MODEL), row),
        out_shape=jax.ShapeDtypeStruct((t, D_MODEL), BF16),
        compiler_params=pltpu.CompilerParams(
            dimension_semantics=("arbitrary",), vmem_limit_bytes=VMEM_LIMIT_BYTES),
        name="outproj",
    )(x2, attn, conv, w_out, w_out, g, b)


def _ffn_kernel(x_ref, wa_ref, wg_ref, cwa_ref, cwg_ref, wd_ref, g_ref, b_ref,
                o_ref, carry_a, carry_g, *, tiles_per_seq):
    i = pl.program_id(0)
    j = pl.program_id(1)
    tm = x_ref.shape[0]
    sb = SB_FFN
    nsub = tm // sb

    @pl.when(i % tiles_per_seq == 0)
    def _():
        carry_a[j] = jnp.zeros(carry_a.shape[1:], F32)
        carry_g[j] = jnp.zeros(carry_g.shape[1:], F32)

    @pl.when(j == 0)
    def _():
        o_ref[...] = DEEPNORM_ALPHA * x_ref[...].astype(F32)

    def up(r):
        xs = x_ref[r * sb:(r + 1) * sb, :]
        return _dot(xs, wa_ref[...]), _dot(xs, wg_ref[...])

    halo_a, halo_g = carry_a[j], carry_g[j]
    nxt = up(0)
    for r in range(nsub):
        ua, ug = nxt
        if r + 1 < nsub:
            nxt = up(r + 1)
        a = _causal_conv3(ua, halo_a, cwa_ref)
        gate = _causal_conv3(ug, halo_g, cwg_ref)
        halo_a, halo_g = ua[sb - SUBLANES:, :], ug[sb - SUBLANES:, :]
        h = (a * jax.nn.sigmoid(a) * gate).astype(BF16)
        o_ref[r * sb:(r + 1) * sb, :] += _dot(h, wd_ref[...])
    carry_a[j] = halo_a
    carry_g[j] = halo_g

    @pl.when(j == pl.num_programs(1) - 1)
    def _():
        o_ref[...] = _layer_norm(o_ref[...], g_ref[...], b_ref[...])


def _ffn(x1, w_up, conv_w, w_down, g, b, seq):
    t = x1.shape[0]
    tm, tf = TM_FFN, TF_FFN
    nf = D_FF // tf
    const = lambda i, j: (0, 0)
    return pl.pallas_call(
        functools.partial(_ffn_kernel, tiles_per_seq=seq // tm),
        grid=(t // tm, nf),
        in_specs=[
            pl.BlockSpec((tm, D_MODEL), lambda i, j: (i, 0)),
            pl.BlockSpec((D_MODEL, tf), lambda i, j: (0, j)),
            pl.BlockSpec((D_MODEL, tf), lambda i, j: (0, j + nf)),
            pl.BlockSpec((CONV_K, tf), lambda i, j: (0, j)),
            pl.BlockSpec((CONV_K, tf), lambda i, j: (0, j + nf)),
            pl.BlockSpec((tf, D_MODEL), lambda i, j: (j, 0)),
            pl.BlockSpec((1, D_MODEL), const),
            pl.BlockSpec((1, D_MODEL), const),
        ],
        out_specs=pl.BlockSpec((tm, D_MODEL), lambda i, j: (i, 0)),
        out_shape=jax.ShapeDtypeStruct((t, D_MODEL), F32),
        scratch_shapes=[pltpu.VMEM((nf, SUBLANES, tf), F32),
                        pltpu.VMEM((nf, SUBLANES, tf), F32)],
        compiler_params=pltpu.CompilerParams(
            dimension_semantics=("arbitrary", "arbitrary"), vmem_limit_bytes=VMEM_LIMIT_BYTES),
        name="ffn",
    )(x1, w_up, w_up, conv_w, conv_w, w_down, g, b)


def _rope_tables(seq):
    half = HEAD_DIM // 2
    inv_freq = ROPE_THETA ** (-jnp.arange(half, dtype=F32) / half)
    ang = jnp.arange(seq).astype(F32)[:, None] * inv_freq[None, :]
    cos = jnp.cos(ang)
    sin = jnp.sin(ang)
    reps = LANES // HEAD_DIM
    return (jnp.tile(jnp.concatenate([cos, cos], axis=-1), (1, reps)),
            jnp.tile(jnp.concatenate([-sin, sin], axis=-1), (1, reps)))


def kernel(x, w_in, attn_sinks, short_conv_w, w_out, ln1_g, ln1_b,
           ffn_w_up, ffn_conv_w, ffn_w_down, ln2_g, ln2_b):
    b, s, d = x.shape
    assert d == D_MODEL and s % max(TM_IN, TQ_ATTN, TM_FFN) == 0
    cos_t, sin_t = _rope_tables(s)
    kv_end = ATTN_WIDTH + 2 * KV_WIDTH
    h = x.reshape(b * s, d)
    for l in range(w_in.shape[0]):
        wq = w_in[l, :, :ATTN_WIDTH].astype(BF16)
        wkv = w_in[l, :, ATTN_WIDTH:kv_end].astype(BF16)
        wbch = w_in[l, :, kv_end:].astype(BF16)
        q, kd, vd, conv = _inproj(h, wq, wkv, wbch, cos_t, sin_t, short_conv_w[l], s)
        attn = _attn(attn_sinks[l], q, kd, vd, s)
        x1 = _outproj(h, attn, conv, w_out[l].astype(BF16), ln1_g[l][None], ln1_b[l][None])
        h = _ffn(x1, ffn_w_up[l].astype(BF16), ffn_conv_w[l], ffn_w_down[l].astype(BF16),
                 ln2_g[l][None], ln2_b[l][None], s)
    return h.reshape(b, s, d)
```

```python
import functools

import jax
import jax.numpy as jnp
from jax import lax
from jax.experimental import pallas as pl
from jax.experimental.pallas import tpu as pltpu

D_MODEL = 2048
HEAD_DIM = 64
N_Q_HEADS = 16
N_KV_HEADS = 2
Q_PER_KV = N_Q_HEADS // N_KV_HEADS
ATTN_WIDTH = N_Q_HEADS * HEAD_DIM
KV_WIDTH = N_KV_HEADS * HEAD_DIM
WINDOW = 128
BLOCK = 128
ROPE_THETA = 10000.0
CONV_WIDTH = D_MODEL - ATTN_WIDTH
CONV_K = 3
D_FF = 5632
LN_EPS = 1e-5
DEPTH = 1
DEEPNORM_ALPHA = (2 * DEPTH) ** 0.25
NEG_INF = -1e30

LANES = 128
SUBLANES = 8
VMEM_BYTES_V7X = 64 * 1024 * 1024
VMEM_LIMIT_BYTES = VMEM_BYTES_V7X - 2 * 1024 * 1024

TM_IN = 512
SB_IN = 256
TQ_ATTN = 512
TM_OUT = 1024
SB_OUT = 256
TM_FFN = 1024
TF_FFN = 512
SB_FFN = 512
SB_FFN_LAST = 256

BF16 = jnp.bfloat16
F32 = jnp.float32


def _dot(a, b):
    return jnp.dot(a, b, preferred_element_type=F32)


def _shift_rows(u, halo, k):
    r = pltpu.roll(u, k, 0)
    hr = pltpu.roll(halo, k, 0)
    row = lax.broadcasted_iota(jnp.int32, hr.shape, 0)
    head = jnp.where(row < k, hr, r[:SUBLANES])
    return jnp.concatenate([head, r[SUBLANES:]], axis=0)


def _causal_conv3(u, halo, w_ref):
    return (w_ref[2:3, :] * u
            + w_ref[1:2, :] * _shift_rows(u, halo, 1)
            + w_ref[0:1, :] * _shift_rows(u, halo, 2))


def _layer_norm(z, g, b):
    mu = jnp.mean(z, axis=-1, keepdims=True)
    zc = z - mu
    var = jnp.mean(zc * zc, axis=-1, keepdims=True)
    return zc * lax.rsqrt(var + LN_EPS) * g + b


def _inproj_kernel(x_ref, wq_ref, wkv_ref, wbch_ref, cos_ref, sin_ref, cw_ref,
                   q_ref, kd_ref, vd_ref, conv_ref, carry_ref, *, tiles_per_seq):
    i = pl.program_id(0)
    tm = x_ref.shape[0]
    sb = SB_IN
    nsub = tm // sb
    lane = lax.broadcasted_iota(jnp.int32, (sb, LANES), 1)
    first_half = (lane & (HEAD_DIM // 2)) == 0
    low_head = lane < HEAD_DIM
    scale = HEAD_DIM ** -0.5

    @pl.when(i % tiles_per_seq == 0)
    def _():
        carry_ref[...] = jnp.zeros_like(carry_ref)

    def dup_heads(t):
        r = pltpu.roll(t, HEAD_DIM, 1)
        return jnp.where(low_head, t, r), jnp.where(low_head, r, t)

    def project(r):
        xb = x_ref[r * sb:(r + 1) * sb, :].astype(BF16)
        return _dot(xb, wbch_ref[...]), _dot(xb, wq_ref[...]), _dot(xb, wkv_ref[...])

    halo = carry_ref[...]
    nxt = project(0)
    for r in range(nsub):
        rows = slice(r * sb, (r + 1) * sb)
        bch, q, kv = nxt
        if r + 1 < nsub:
            nxt = project(r + 1)
        cos = cos_ref[rows, :]
        sin = sin_ref[rows, :]

        def rope(t):
            partner = jnp.where(first_half, pltpu.roll(t, LANES - HEAD_DIM // 2, 1),
                                pltpu.roll(t, HEAD_DIM // 2, 1))
            return t * cos + partner * sin

        gate_b = bch[:, :CONV_WIDTH]
        u = bch[:, CONV_WIDTH:2 * CONV_WIDTH] * bch[:, 2 * CONV_WIDTH:]
        conv_ref[rows, :] = (gate_b * _causal_conv3(u, halo, cw_ref)).astype(BF16)
        halo = u[sb - SUBLANES:, :]
        for c in range(ATTN_WIDTH // LANES):
            sl = slice(c * LANES, (c + 1) * LANES)
            q_ref[rows, sl] = (rope(q[:, sl]) * scale).astype(BF16)
        k0, k1 = dup_heads(rope(kv[:, :KV_WIDTH]))
        v0, v1 = dup_heads(kv[:, KV_WIDTH:])
        kd_ref[rows, :LANES] = k0.astype(BF16)
        kd_ref[rows, LANES:] = k1.astype(BF16)
        vd_ref[rows, :LANES] = v0.astype(BF16)
        vd_ref[rows, LANES:] = v1.astype(BF16)
    carry_ref[...] = halo


def _inproj(x2, wq, wkv, wbch, cos_t, sin_t, conv_w, seq):
    t = x2.shape[0]
    tm = TM_IN
    tiles_per_seq = seq // tm
    const = lambda i: (0, 0)
    resident = functools.partial(pl.BlockSpec, index_map=const, pipeline_mode=pl.Buffered(1))
    row = lambda i: (i, 0)
    return pl.pallas_call(
        functools.partial(_inproj_kernel, tiles_per_seq=tiles_per_seq),
        grid=(t // tm,),
        in_specs=[
            pl.BlockSpec((tm, D_MODEL), row),
            resident((D_MODEL, ATTN_WIDTH)),
            resident((D_MODEL, 2 * KV_WIDTH)),
            resident((D_MODEL, 3 * CONV_WIDTH)),
            pl.BlockSpec((tm, LANES), lambda i: (i % tiles_per_seq, 0)),
            pl.BlockSpec((tm, LANES), lambda i: (i % tiles_per_seq, 0)),
            pl.BlockSpec((CONV_K, CONV_WIDTH), const),
        ],
        out_specs=[
            pl.BlockSpec((tm, ATTN_WIDTH), row),
            pl.BlockSpec((tm, 2 * LANES), row),
            pl.BlockSpec((tm, 2 * LANES), row),
            pl.BlockSpec((tm, CONV_WIDTH), row),
        ],
        out_shape=[
            jax.ShapeDtypeStruct((t, ATTN_WIDTH), BF16),
            jax.ShapeDtypeStruct((t, 2 * LANES), BF16),
            jax.ShapeDtypeStruct((t, 2 * LANES), BF16),
            jax.ShapeDtypeStruct((t, CONV_WIDTH), BF16),
        ],
        scratch_shapes=[pltpu.VMEM((SUBLANES, CONV_WIDTH), F32)],
        compiler_params=pltpu.CompilerParams(
            dimension_semantics=("arbitrary",), vmem_limit_bytes=VMEM_LIMIT_BYTES),
        name="inproj",
    )(x2, wq, wkv, wbch, cos_t, sin_t, conv_w)


def _attn_kernel(sink_ref, q_ref, kc_ref, kp_ref, vc_ref, vp_ref, o_ref, *, blocks_per_seq):
    i = pl.program_id(0)
    tq = q_ref.shape[0]
    nb = tq // BLOCK
    kfull = jnp.concatenate([kp_ref[...], kc_ref[...]], axis=0)
    vfull = jnp.concatenate([vp_ref[...], vc_ref[...]], axis=0)
    assert WINDOW == BLOCK and blocks_per_seq % nb == 0
    qi = lax.broadcasted_iota(jnp.int32, (BLOCK, BLOCK), 0)
    cj = lax.broadcasted_iota(jnp.int32, (BLOCK, BLOCK), 1)
    from_prev = cj > qi
    lane = lax.broadcasted_iota(jnp.int32, (BLOCK, LANES), 1)
    low_head = lane < HEAD_DIM
    zero = jnp.zeros((BLOCK, LANES), BF16)

    for b in range(nb):
        rows = slice(b * BLOCK, (b + 1) * BLOCK)
        band = slice(b * BLOCK, (b + 2) * BLOCK)
        no_prev = from_prev & ((i * nb) % blocks_per_seq == 0) if b == 0 else None
        for kvh in range(N_KV_HEADS):
            cols = slice(kvh * LANES, (kvh + 1) * LANES)
            k2 = kfull[band, cols]
            v2 = vfull[band, cols]
            qs = []
            for p in range(Q_PER_KV // 2):
                c0 = kvh * Q_PER_KV * HEAD_DIM + p * LANES
                qp = q_ref[rows, c0:c0 + LANES]
                qs.append(jnp.where(low_head, qp, zero))
                qs.append(jnp.where(low_head, zero, qp))
            s_all = lax.dot_general(jnp.concatenate(qs, axis=0), k2,
                                    (((1,), (1,)), ((), ())), preferred_element_type=F32)
            ps, inv_l = [], []
            for g in range(Q_PER_KV):
                sg = s_all[g * BLOCK:(g + 1) * BLOCK]
                s = jnp.where(from_prev, sg[:, :BLOCK], sg[:, BLOCK:])
                if no_prev is not None:
                    s = jnp.where(no_prev, NEG_INF, s)
                sink = sink_ref[kvh * Q_PER_KV + g]
                m = jnp.maximum(jnp.max(s, axis=-1, keepdims=True), sink)
                pr = jnp.exp(s - m)
                l = jnp.sum(pr, axis=-1, keepdims=True) + jnp.exp(sink - m)
                prb = pr.astype(BF16)
                ps.append(jnp.concatenate([jnp.where(from_prev, prb, zero),
                                           jnp.where(from_prev, zero, prb)], axis=1))
                inv_l.append(1.0 / l)
            o_all = _dot(jnp.concatenate(ps, axis=0), v2)
            for p in range(Q_PER_KV // 2):
                g0, g1 = 2 * p, 2 * p + 1
                o0 = o_all[g0 * BLOCK:(g0 + 1) * BLOCK] * inv_l[g0]
                o1 = o_all[g1 * BLOCK:(g1 + 1) * BLOCK] * inv_l[g1]
                c0 = kvh * Q_PER_KV * HEAD_DIM + p * LANES
                o_ref[rows, c0:c0 + LANES] = jnp.where(low_head, o0, o1).astype(BF16)


def _attn(sinks, q, kd, vd, seq):
    t = q.shape[0]
    tq = TQ_ATTN
    nb = tq // BLOCK
    row = lambda i: (i, 0)
    prev = lambda i: (jnp.maximum(i * nb - 1, 0), 0)
    return pl.pallas_call(
        functools.partial(_attn_kernel, blocks_per_seq=seq // BLOCK),
        grid=(t // tq,),
        in_specs=[
            pl.BlockSpec(memory_space=pltpu.SMEM),
            pl.BlockSpec((tq, ATTN_WIDTH), row),
            pl.BlockSpec((tq, 2 * LANES), row),
            pl.BlockSpec((BLOCK, 2 * LANES), prev),
            pl.BlockSpec((tq, 2 * LANES), row),
            pl.BlockSpec((BLOCK, 2 * LANES), prev),
        ],
        out_specs=pl.BlockSpec((tq, ATTN_WIDTH), row),
        out_shape=jax.ShapeDtypeStruct((t, ATTN_WIDTH), BF16),
        compiler_params=pltpu.CompilerParams(
            dimension_semantics=("arbitrary",), vmem_limit_bytes=VMEM_LIMIT_BYTES),
        name="attn",
    )(sinks, q, kd, kd, vd, vd)


def _outproj_kernel(x_ref, a_ref, c_ref, wa_ref, wc_ref, g_ref, b_ref, o_ref):
    sb = SB_OUT
    nsub = x_ref.shape[0] // sb

    def project(r):
        rows = slice(r * sb, (r + 1) * sb)
        return _dot(a_ref[rows, :], wa_ref[...]) + _dot(c_ref[rows, :], wc_ref[...])

    nxt = project(0)
    for r in range(nsub):
        rows = slice(r * sb, (r + 1) * sb)
        y = nxt
        if r + 1 < nsub:
            nxt = project(r + 1)
        z = DEEPNORM_ALPHA * x_ref[rows, :] + y
        o_ref[rows, :] = _layer_norm(z, g_ref[...], b_ref[...]).astype(o_ref.dtype)


def _outproj(x2, attn, conv, w_out, g, b):
    t = x2.shape[0]
    tm = TM_OUT
    row = lambda i: (i, 0)
    const = lambda i: (0, 0)
    return pl.pallas_call(
        _outproj_kernel,
        grid=(t // tm,),
        in_specs=[
            pl.BlockSpec((tm, D_MODEL), row),
            pl.BlockSpec((tm, ATTN_WIDTH), row),
            pl.BlockSpec((tm, CONV_WIDTH), row),
            pl.BlockSpec((ATTN_WIDTH, D_MODEL), const, pipeline_mode=pl.Buffered(1)),
            pl.BlockSpec((CONV_WIDTH, D_MODEL), lambda i: (1, 0), pipeline_mode=pl.Buffered(1)),
            pl.BlockSpec((1, D_MODEL), const),
            pl.BlockSpec((1, D_MODEL), const),
        ],
        out_specs=pl.BlockSpec((tm, D_MODEL), row),
        out_shape=jax.ShapeDtypeStruct((t, D_MODEL), BF16),
        compiler_params=pltpu.CompilerParams(
            dimension_semantics=("arbitrary",), vmem_limit_bytes=VMEM_LIMIT_BYTES),
        name="outproj",
    )(x2, attn, conv, w_out, w_out, g, b)


def _ffn_kernel(x_ref, wa_ref, wg_ref, cwa_ref, cwg_ref, wd_ref, g_ref, b_ref,
                o_ref, carry_a, carry_g, *, tiles_per_seq):
    i = pl.program_id(0)
    j = pl.program_id(1)
    tm = x_ref.shape[0]
    sb = SB_FFN
    nsub = tm // sb

    @pl.when(i % tiles_per_seq == 0)
    def _():
        carry_a[j] = jnp.zeros(carry_a.shape[1:], F32)
        carry_g[j] = jnp.zeros(carry_g.shape[1:], F32)

    def up(r):
        xs = x_ref[r * sb:(r + 1) * sb, :]
        return _dot(xs, wa_ref[...]), _dot(xs, wg_ref[...])

    def body(first, last):
        piece = SB_FFN_LAST if last else sb
        halo_a, halo_g = carry_a[j], carry_g[j]
        nxt = up(0)
        for r in range(nsub):
            ua, ug = nxt
            if r + 1 < nsub:
                nxt = up(r + 1)
            a = _causal_conv3(ua, halo_a, cwa_ref)
            gate = _causal_conv3(ug, halo_g, cwg_ref)
            halo_a, halo_g = ua[sb - SUBLANES:, :], ug[sb - SUBLANES:, :]
            h = (a * jax.nn.sigmoid(a) * gate).astype(BF16)
            for p in range(sb // piece):
                rows = slice(r * sb + p * piece, r * sb + (p + 1) * piece)
                if first:
                    acc = DEEPNORM_ALPHA * x_ref[rows, :].astype(F32)
                else:
                    acc = o_ref[rows, :]
                acc = acc + _dot(h[p * piece:(p + 1) * piece, :], wd_ref[...])
                if last:
                    acc = _layer_norm(acc, g_ref[...], b_ref[...])
                o_ref[rows, :] = acc
        carry_a[j] = halo_a
        carry_g[j] = halo_g

    last_j = pl.num_programs(1) - 1
    pl.when(j == 0)(functools.partial(body, True, False))
    pl.when((j > 0) & (j < last_j))(functools.partial(body, False, False))
    pl.when(j == last_j)(functools.partial(body, False, True))


def _ffn(x1, w_up, conv_w, w_down, g, b, seq):
    t = x1.shape[0]
    tm, tf = TM_FFN, TF_FFN
    nf = D_FF // tf
    assert nf >= 2
    const = lambda i, j: (0, 0)
    return pl.pallas_call(
        functools.partial(_ffn_kernel, tiles_per_seq=seq // tm),
        grid=(t // tm, nf),
        in_specs=[
            pl.BlockSpec((tm, D_MODEL), lambda i, j: (i, 0)),
            pl.BlockSpec((D_MODEL, tf), lambda i, j: (0, j)),
            pl.BlockSpec((D_MODEL, tf), lambda i, j: (0, j + nf)),
            pl.BlockSpec((CONV_K, tf), lambda i, j: (0, j)),
            pl.BlockSpec((CONV_K, tf), lambda i, j: (0, j + nf)),
            pl.BlockSpec((tf, D_MODEL), lambda i, j: (j, 0)),
            pl.BlockSpec((1, D_MODEL), const),
            pl.BlockSpec((1, D_MODEL), const),
        ],
        out_specs=pl.BlockSpec((tm, D_MODEL), lambda i, j: (i, 0)),
        out_shape=jax.ShapeDtypeStruct((t, D_MODEL), F32),
        scratch_shapes=[pltpu.VMEM((nf, SUBLANES, tf), F32),
                        pltpu.VMEM((nf, SUBLANES, tf), F32)],
        compiler_params=pltpu.CompilerParams(
            dimension_semantics=("arbitrary", "arbitrary"), vmem_limit_bytes=VMEM_LIMIT_BYTES),
        name="ffn",
    )(x1, w_up, w_up, conv_w, conv_w, w_down, g, b)


def _rope_tables(seq):
    half = HEAD_DIM // 2
    inv_freq = ROPE_THETA ** (-jnp.arange(half, dtype=F32) / half)
    ang = jnp.arange(seq).astype(F32)[:, None] * inv_freq[None, :]
    cos = jnp.cos(ang)
    sin = jnp.sin(ang)
    reps = LANES // HEAD_DIM
    return (jnp.tile(jnp.concatenate([cos, cos], axis=-1), (1, reps)),
            jnp.tile(jnp.concatenate([-sin, sin], axis=-1), (1, reps)))


def kernel(x, w_in, attn_sinks, short_conv_w, w_out, ln1_g, ln1_b,
           ffn_w_up, ffn_conv_w, ffn_w_down, ln2_g, ln2_b):
    b, s, d = x.shape
    assert d == D_MODEL and s % max(TM_IN, TQ_ATTN, TM_FFN) == 0
    cos_t, sin_t = _rope_tables(s)
    kv_end = ATTN_WIDTH + 2 * KV_WIDTH
    h = x.reshape(b * s, d)
    for l in range(w_in.shape[0]):
        wq = w_in[l, :, :ATTN_WIDTH].astype(BF16)
        wkv = w_in[l, :, ATTN_WIDTH:kv_end].astype(BF16)
        wbch = w_in[l, :, kv_end:].astype(BF16)
        q, kd, vd, conv = _inproj(h, wq, wkv, wbch, cos_t, sin_t, short_conv_w[l], s)
        attn = _attn(attn_sinks[l], q, kd, vd, s)
        x1 = _outproj(h, attn, conv, w_out[l], ln1_g[l][None], ln1_b[l][None])
        h = _ffn(x1, ffn_w_up[l], ffn_conv_w[l], ffn_w_down[l],
                 ln2_g[l][None], ln2_b[l][None], s)
    return h.reshape(b, s, d)
```

```python
import functools

import jax
import jax.numpy as jnp
from jax import lax
from jax.experimental import pallas as pl
from jax.experimental.pallas import tpu as pltpu

D_MODEL = 2048
HEAD_DIM = 64
N_Q_HEADS = 16
N_KV_HEADS = 2
Q_PER_KV = N_Q_HEADS // N_KV_HEADS
ATTN_WIDTH = N_Q_HEADS * HEAD_DIM
KV_WIDTH = N_KV_HEADS * HEAD_DIM
WINDOW = 128
BLOCK = 128
ROPE_THETA = 10000.0
CONV_WIDTH = D_MODEL - ATTN_WIDTH
CONV_K = 3
D_FF = 5632
LN_EPS = 1e-5
DEPTH = 1
DEEPNORM_ALPHA = (2 * DEPTH) ** 0.25
NEG_INF = -1e30

LANES = 128
SUBLANES = 8
VMEM_BYTES_V7X = 64 * 1024 * 1024
VMEM_LIMIT_BYTES = VMEM_BYTES_V7X - 2 * 1024 * 1024

TM_IN = 512
SB_IN = 256
TQ_ATTN = 512
TM_OUT = 1024
SB_OUT = 256
TM_FFN = 1024
TF_FFN = 1024
SB_FFN = 512
SB_FFN_LAST = 256

BF16 = jnp.bfloat16
F32 = jnp.float32


def _dot(a, b):
    return jnp.dot(a, b, preferred_element_type=F32)


def _shift_rows(u, halo, k):
    r = pltpu.roll(u, k, 0)
    hr = pltpu.roll(halo, k, 0)
    row = lax.broadcasted_iota(jnp.int32, hr.shape, 0)
    head = jnp.where(row < k, hr, r[:SUBLANES])
    return jnp.concatenate([head, r[SUBLANES:]], axis=0)


def _causal_conv3(u, halo, w_ref):
    return (w_ref[2:3, :] * u
            + w_ref[1:2, :] * _shift_rows(u, halo, 1)
            + w_ref[0:1, :] * _shift_rows(u, halo, 2))


def _layer_norm(z, g, b):
    mu = jnp.mean(z, axis=-1, keepdims=True)
    zc = z - mu
    var = jnp.mean(zc * zc, axis=-1, keepdims=True)
    return zc * lax.rsqrt(var + LN_EPS) * g + b


def _inproj_kernel(x_ref, w_ref, cos_ref, sin_ref, cw_ref,
                   q_ref, kd_ref, vd_ref, conv_ref, carry_ref, *, tiles_per_seq):
    kv_end = ATTN_WIDTH + 2 * KV_WIDTH
    wq_ref = w_ref.at[:, :ATTN_WIDTH]
    wkv_ref = w_ref.at[:, ATTN_WIDTH:kv_end]
    wbch_ref = w_ref.at[:, kv_end:]
    i = pl.program_id(0)
    tm = x_ref.shape[0]
    sb = SB_IN
    nsub = tm // sb
    lane = lax.broadcasted_iota(jnp.int32, (sb, LANES), 1)
    first_half = (lane & (HEAD_DIM // 2)) == 0
    low_head = lane < HEAD_DIM
    scale = HEAD_DIM ** -0.5

    @pl.when(i % tiles_per_seq == 0)
    def _():
        carry_ref[...] = jnp.zeros_like(carry_ref)

    def dup_heads(t):
        r = pltpu.roll(t, HEAD_DIM, 1)
        return jnp.where(low_head, t, r), jnp.where(low_head, r, t)

    def project(r):
        xb = x_ref[r * sb:(r + 1) * sb, :].astype(BF16)
        return _dot(xb, wbch_ref[...]), _dot(xb, wq_ref[...]), _dot(xb, wkv_ref[...])

    halo = carry_ref[...]
    nxt = project(0)
    for r in range(nsub):
        rows = slice(r * sb, (r + 1) * sb)
        bch, q, kv = nxt
        if r + 1 < nsub:
            nxt = project(r + 1)
        cos = cos_ref[rows, :]
        sin = sin_ref[rows, :]

        def rope(t):
            partner = jnp.where(first_half, pltpu.roll(t, LANES - HEAD_DIM // 2, 1),
                                pltpu.roll(t, HEAD_DIM // 2, 1))
            return t * cos + partner * sin

        gate_b = bch[:, :CONV_WIDTH]
        u = bch[:, CONV_WIDTH:2 * CONV_WIDTH] * bch[:, 2 * CONV_WIDTH:]
        conv_ref[rows, :] = (gate_b * _causal_conv3(u, halo, cw_ref)).astype(BF16)
        halo = u[sb - SUBLANES:, :]
        for c in range(ATTN_WIDTH // LANES):
            sl = slice(c * LANES, (c + 1) * LANES)
            q_ref[rows, sl] = (rope(q[:, sl]) * scale).astype(BF16)
        k0, k1 = dup_heads(rope(kv[:, :KV_WIDTH]))
        v0, v1 = dup_heads(kv[:, KV_WIDTH:])
        kd_ref[rows, :LANES] = k0.astype(BF16)
        kd_ref[rows, LANES:] = k1.astype(BF16)
        vd_ref[rows, :LANES] = v0.astype(BF16)
        vd_ref[rows, LANES:] = v1.astype(BF16)
    carry_ref[...] = halo


def _inproj(x2, w_in, cos_t, sin_t, conv_w, seq):
    t = x2.shape[0]
    tm = TM_IN
    tiles_per_seq = seq // tm
    const = lambda i: (0, 0)
    row = lambda i: (i, 0)
    return pl.pallas_call(
        functools.partial(_inproj_kernel, tiles_per_seq=tiles_per_seq),
        grid=(t // tm,),
        in_specs=[
            pl.BlockSpec((tm, D_MODEL), row),
            pl.BlockSpec(w_in.shape, const, pipeline_mode=pl.Buffered(1)),
            pl.BlockSpec((tm, LANES), lambda i: (i % tiles_per_seq, 0)),
            pl.BlockSpec((tm, LANES), lambda i: (i % tiles_per_seq, 0)),
            pl.BlockSpec((CONV_K, CONV_WIDTH), const),
        ],
        out_specs=[
            pl.BlockSpec((tm, ATTN_WIDTH), row),
            pl.BlockSpec((tm, 2 * LANES), row),
            pl.BlockSpec((tm, 2 * LANES), row),
            pl.BlockSpec((tm, CONV_WIDTH), row),
        ],
        out_shape=[
            jax.ShapeDtypeStruct((t, ATTN_WIDTH), BF16),
            jax.ShapeDtypeStruct((t, 2 * LANES), BF16),
            jax.ShapeDtypeStruct((t, 2 * LANES), BF16),
            jax.ShapeDtypeStruct((t, CONV_WIDTH), BF16),
        ],
        scratch_shapes=[pltpu.VMEM((SUBLANES, CONV_WIDTH), F32)],
        compiler_params=pltpu.CompilerParams(
            dimension_semantics=("arbitrary",), vmem_limit_bytes=VMEM_LIMIT_BYTES),
        name="inproj",
    )(x2, w_in, cos_t, sin_t, conv_w)


def _attn_kernel(sink_ref, q_ref, kc_ref, kp_ref, vc_ref, vp_ref, wu_ref, wd_ref,
                 o_ref, wu16_ref, wd16_ref, *, blocks_per_seq):
    i = pl.program_id(0)
    tq = q_ref.shape[0]
    nb = tq // BLOCK
    wu16_ref[...] = wu_ref[...].astype(BF16)
    wd16_ref[...] = wd_ref[...].astype(BF16)
    kfull = jnp.concatenate([kp_ref[...], kc_ref[...]], axis=0)
    vfull = jnp.concatenate([vp_ref[...], vc_ref[...]], axis=0)
    assert WINDOW == BLOCK and blocks_per_seq % nb == 0
    qi = lax.broadcasted_iota(jnp.int32, (BLOCK, BLOCK), 0)
    cj = lax.broadcasted_iota(jnp.int32, (BLOCK, BLOCK), 1)
    from_prev = cj > qi
    lane = lax.broadcasted_iota(jnp.int32, (BLOCK, LANES), 1)
    low_head = lane < HEAD_DIM
    zero = jnp.zeros((BLOCK, LANES), BF16)

    for b in range(nb):
        rows = slice(b * BLOCK, (b + 1) * BLOCK)
        band = slice(b * BLOCK, (b + 2) * BLOCK)
        no_prev = from_prev & ((i * nb) % blocks_per_seq == 0) if b == 0 else None
        for kvh in range(N_KV_HEADS):
            cols = slice(kvh * LANES, (kvh + 1) * LANES)
            k2 = kfull[band, cols]
            v2 = vfull[band, cols]
            qs = []
            for p in range(Q_PER_KV // 2):
                c0 = kvh * Q_PER_KV * HEAD_DIM + p * LANES
                qp = q_ref[rows, c0:c0 + LANES]
                qs.append(jnp.where(low_head, qp, zero))
                qs.append(jnp.where(low_head, zero, qp))
            s_all = lax.dot_general(jnp.concatenate(qs, axis=0), k2,
                                    (((1,), (1,)), ((), ())), preferred_element_type=F32)
            ps, inv_l = [], []
            for g in range(Q_PER_KV):
                sg = s_all[g * BLOCK:(g + 1) * BLOCK]
                s = jnp.where(from_prev, sg[:, :BLOCK], sg[:, BLOCK:])
                if no_prev is not None:
                    s = jnp.where(no_prev, NEG_INF, s)
                sink = sink_ref[kvh * Q_PER_KV + g]
                m = jnp.maximum(jnp.max(s, axis=-1, keepdims=True), sink)
                pr = jnp.exp(s - m)
                l = jnp.sum(pr, axis=-1, keepdims=True) + jnp.exp(sink - m)
                prb = pr.astype(BF16)
                ps.append(jnp.concatenate([jnp.where(from_prev, prb, zero),
                                           jnp.where(from_prev, zero, prb)], axis=1))
                inv_l.append(1.0 / l)
            o_all = _dot(jnp.concatenate(ps, axis=0), v2)
            for p in range(Q_PER_KV // 2):
                g0, g1 = 2 * p, 2 * p + 1
                o0 = o_all[g0 * BLOCK:(g0 + 1) * BLOCK] * inv_l[g0]
                o1 = o_all[g1 * BLOCK:(g1 + 1) * BLOCK] * inv_l[g1]
                c0 = kvh * Q_PER_KV * HEAD_DIM + p * LANES
                o_ref[rows, c0:c0 + LANES] = jnp.where(low_head, o0, o1).astype(BF16)


def _attn(sinks, q, kd, vd, w_up, w_down, seq):
    t = q.shape[0]
    tq = TQ_ATTN
    nb = tq // BLOCK
    steps = t // tq
    bf16_rows = 2 * SUBLANES
    ru, rd = w_up.shape[0] // steps, w_down.shape[0] // steps
    assert ru * steps == w_up.shape[0] and rd * steps == w_down.shape[0]
    assert ru % bf16_rows == 0 and rd % bf16_rows == 0
    row = lambda i: (i, 0)
    prev = lambda i: (jnp.maximum(i * nb - 1, 0), 0)
    return pl.pallas_call(
        functools.partial(_attn_kernel, blocks_per_seq=seq // BLOCK),
        grid=(steps,),
        in_specs=[
            pl.BlockSpec(memory_space=pltpu.SMEM),
            pl.BlockSpec((tq, ATTN_WIDTH), row),
            pl.BlockSpec((tq, 2 * LANES), row),
            pl.BlockSpec((BLOCK, 2 * LANES), prev),
            pl.BlockSpec((tq, 2 * LANES), row),
            pl.BlockSpec((BLOCK, 2 * LANES), prev),
            pl.BlockSpec((ru, w_up.shape[1]), row),
            pl.BlockSpec((rd, w_down.shape[1]), row),
        ],
        out_specs=[
            pl.BlockSpec((tq, ATTN_WIDTH), row),
            pl.BlockSpec((ru, w_up.shape[1]), row),
            pl.BlockSpec((rd, w_down.shape[1]), row),
        ],
        out_shape=[
            jax.ShapeDtypeStruct((t, ATTN_WIDTH), BF16),
            jax.ShapeDtypeStruct(w_up.shape, BF16),
            jax.ShapeDtypeStruct(w_down.shape, BF16),
        ],
        compiler_params=pltpu.CompilerParams(
            dimension_semantics=("arbitrary",), vmem_limit_bytes=VMEM_LIMIT_BYTES),
        name="attn",
    )(sinks, q, kd, kd, vd, vd, w_up, w_down)


def _outproj_kernel(x_ref, a_ref, c_ref, wa_ref, wc_ref, g_ref, b_ref, o_ref):
    sb = SB_OUT
    nsub = x_ref.shape[0] // sb

    def project(r):
        rows = slice(r * sb, (r + 1) * sb)
        return _dot(a_ref[rows, :], wa_ref[...]) + _dot(c_ref[rows, :], wc_ref[...])

    nxt = project(0)
    for r in range(nsub):
        rows = slice(r * sb, (r + 1) * sb)
        y = nxt
        if r + 1 < nsub:
            nxt = project(r + 1)
        z = DEEPNORM_ALPHA * x_ref[rows, :] + y
        o_ref[rows, :] = _layer_norm(z, g_ref[...], b_ref[...]).astype(o_ref.dtype)


def _outproj(x2, attn, conv, w_out, g, b):
    t = x2.shape[0]
    tm = TM_OUT
    row = lambda i: (i, 0)
    const = lambda i: (0, 0)
    return pl.pallas_call(
        _outproj_kernel,
        grid=(t // tm,),
        in_specs=[
            pl.BlockSpec((tm, D_MODEL), row),
            pl.BlockSpec((tm, ATTN_WIDTH), row),
            pl.BlockSpec((tm, CONV_WIDTH), row),
            pl.BlockSpec((ATTN_WIDTH, D_MODEL), const, pipeline_mode=pl.Buffered(1)),
            pl.BlockSpec((CONV_WIDTH, D_MODEL), lambda i: (1, 0), pipeline_mode=pl.Buffered(1)),
            pl.BlockSpec((1, D_MODEL), const),
            pl.BlockSpec((1, D_MODEL), const),
        ],
        out_specs=pl.BlockSpec((tm, D_MODEL), row),
        out_shape=jax.ShapeDtypeStruct((t, D_MODEL), BF16),
        compiler_params=pltpu.CompilerParams(
            dimension_semantics=("arbitrary",), vmem_limit_bytes=VMEM_LIMIT_BYTES),
        name="outproj",
    )(x2, attn, conv, w_out, w_out, g, b)


def _ffn_kernel(x_ref, wa_ref, wg_ref, cwa_ref, cwg_ref, wd_ref, g_ref, b_ref,
                o_ref, carry_a, carry_g, *, tiles_per_seq, first_width):
    i = pl.program_id(0)
    j = pl.program_id(1)
    tm = x_ref.shape[0]
    sb = SB_FFN
    nsub = tm // sb

    tf = wa_ref.shape[1]

    @pl.when(i % tiles_per_seq == 0)
    def _():
        carry_a[j] = jnp.zeros(carry_a.shape[1:], F32)
        carry_g[j] = jnp.zeros(carry_g.shape[1:], F32)

    def body(first, last):
        width = first_width if first else tf
        cols = slice(0, width)
        wa, wg, wd = wa_ref.at[:, cols], wg_ref.at[:, cols], wd_ref.at[cols, :]
        cwa, cwg = cwa_ref.at[:, cols], cwg_ref.at[:, cols]
        piece = SB_FFN_LAST if last else sb

        def up(r):
            xs = x_ref[r * sb:(r + 1) * sb, :]
            return _dot(xs, wa[...]), _dot(xs, wg[...])

        halo_a, halo_g = carry_a[j][:, :width], carry_g[j][:, :width]
        nxt = up(0)
        for r in range(nsub):
            ua, ug = nxt
            if r + 1 < nsub:
                nxt = up(r + 1)
            a = _causal_conv3(ua, halo_a, cwa)
            gate = _causal_conv3(ug, halo_g, cwg)
            halo_a, halo_g = ua[sb - SUBLANES:, :], ug[sb - SUBLANES:, :]
            h = (a * jax.nn.sigmoid(a) * gate).astype(BF16)
            for p in range(sb // piece):
                rows = slice(r * sb + p * piece, r * sb + (p + 1) * piece)
                if first:
                    acc = DEEPNORM_ALPHA * x_ref[rows, :].astype(F32)
                else:
                    acc = o_ref[rows, :]
                acc = acc + _dot(h[p * piece:(p + 1) * piece, :], wd[...])
                if last:
                    acc = _layer_norm(acc, g_ref[...], b_ref[...])
                o_ref[rows, :] = acc
        carry_a[j, :, :width] = halo_a
        carry_g[j, :, :width] = halo_g

    last_j = pl.num_programs(1) - 1
    pl.when(j == 0)(functools.partial(body, True, False))
    pl.when((j > 0) & (j < last_j))(functools.partial(body, False, False))
    pl.when(j == last_j)(functools.partial(body, False, True))


def _ffn(x1, w_up, conv_w, w_down, g, b, seq):
    t = x1.shape[0]
    tm, tf = TM_FFN, TF_FFN
    nf = pl.cdiv(D_FF, tf)
    assert nf >= 2
    first_width = D_FF - (nf - 1) * tf
    const = lambda i, j: (0, 0)
    assert tf % first_width == 0 and D_FF % first_width == 0 and first_width % LANES == 0
    ratio = tf // first_width
    start = lambda j: jnp.maximum(j * ratio - (ratio - 1), 0) * first_width
    start_g = lambda j: (D_FF // first_width + jnp.maximum(j * ratio - (ratio - 1), 0)) * first_width
    el = pl.Element
    return pl.pallas_call(
        functools.partial(_ffn_kernel, tiles_per_seq=seq // tm, first_width=first_width),
        grid=(t // tm, nf),
        in_specs=[
            pl.BlockSpec((tm, D_MODEL), lambda i, j: (i, 0)),
            pl.BlockSpec((el(D_MODEL), el(tf)), lambda i, j: (0, start(j))),
            pl.BlockSpec((el(D_MODEL), el(tf)), lambda i, j: (0, start_g(j))),
            pl.BlockSpec((el(CONV_K), el(tf)), lambda i, j: (0, start(j))),
            pl.BlockSpec((el(CONV_K), el(tf)), lambda i, j: (0, start_g(j))),
            pl.BlockSpec((el(tf), el(D_MODEL)), lambda i, j: (start(j), 0)),
            pl.BlockSpec((1, D_MODEL), const),
            pl.BlockSpec((1, D_MODEL), const),
        ],
        out_specs=pl.BlockSpec((tm, D_MODEL), lambda i, j: (i, 0)),
        out_shape=jax.ShapeDtypeStruct((t, D_MODEL), F32),
        scratch_shapes=[pltpu.VMEM((nf, SUBLANES, tf), F32),
                        pltpu.VMEM((nf, SUBLANES, tf), F32)],
        compiler_params=pltpu.CompilerParams(
            dimension_semantics=("arbitrary", "arbitrary"), vmem_limit_bytes=VMEM_LIMIT_BYTES),
        name="ffn",
    )(x1, w_up, w_up, conv_w, conv_w, w_down, g, b)


def _rope_tables(seq):
    half = HEAD_DIM // 2
    inv_freq = ROPE_THETA ** (-jnp.arange(half, dtype=F32) / half)
    ang = jnp.arange(seq).astype(F32)[:, None] * inv_freq[None, :]
    cos = jnp.cos(ang)
    sin = jnp.sin(ang)
    reps = LANES // HEAD_DIM
    return (jnp.tile(jnp.concatenate([cos, cos], axis=-1), (1, reps)),
            jnp.tile(jnp.concatenate([-sin, sin], axis=-1), (1, reps)))


def kernel(x, w_in, attn_sinks, short_conv_w, w_out, ln1_g, ln1_b,
           ffn_w_up, ffn_conv_w, ffn_w_down, ln2_g, ln2_b):
    b, s, d = x.shape
    assert d == D_MODEL and s % max(TM_IN, TQ_ATTN, TM_FFN) == 0
    cos_t, sin_t = _rope_tables(s)
    h = x.reshape(b * s, d)
    for l in range(w_in.shape[0]):
        q, kd, vd, conv = _inproj(h, w_in[l].astype(BF16), cos_t, sin_t, short_conv_w[l], s)
        attn, w_up16, w_down16 = _attn(attn_sinks[l], q, kd, vd, ffn_w_up[l], ffn_w_down[l], s)
        x1 = _outproj(h, attn, conv, w_out[l], ln1_g[l][None], ln1_b[l][None])
        h = _ffn(x1, w_up16, ffn_conv_w[l], w_down16, ln2_g[l][None], ln2_b[l][None], s)
    return h.reshape(b, s, d)
```

```python
import functools

import jax
import jax.numpy as jnp
from jax import lax
from jax.experimental import pallas as pl
from jax.experimental.pallas import tpu as pltpu

D_MODEL = 2048
HEAD_DIM = 64
N_Q_HEADS = 16
N_KV_HEADS = 2
Q_PER_KV = N_Q_HEADS // N_KV_HEADS
ATTN_WIDTH = N_Q_HEADS * HEAD_DIM
KV_WIDTH = N_KV_HEADS * HEAD_DIM
WINDOW = 128
BLOCK = 128
ROPE_THETA = 10000.0
CONV_WIDTH = D_MODEL - ATTN_WIDTH
CONV_K = 3
D_FF = 5632
LN_EPS = 1e-5
DEPTH = 1
DEEPNORM_ALPHA = (2 * DEPTH) ** 0.25
NEG_INF = -1e30

LANES = 128
SUBLANES = 8
VMEM_BYTES_V7X = 64 * 1024 * 1024
VMEM_LIMIT_BYTES = VMEM_BYTES_V7X - 2 * 1024 * 1024

TM_IN = 512
SB_IN = 256
TQ_ATTN = 512
TM_OUT = 1024
SB_OUT = 256
TM_FFN = 1024
TF_FFN = 1024
SB_FFN = 512
SB_FFN_LAST = 256

BF16 = jnp.bfloat16
F32 = jnp.float32


def _dot(a, b):
    return jnp.dot(a, b, preferred_element_type=F32)


def _shift_rows(u, halo, k):
    r = pltpu.roll(u, k, 0)
    hr = pltpu.roll(halo, k, 0)
    row = lax.broadcasted_iota(jnp.int32, hr.shape, 0)
    head = jnp.where(row < k, hr, r[:SUBLANES])
    return jnp.concatenate([head, r[SUBLANES:]], axis=0)


def _causal_conv3(u, halo, w_ref):
    return (w_ref[2:3, :] * u
            + w_ref[1:2, :] * _shift_rows(u, halo, 1)
            + w_ref[0:1, :] * _shift_rows(u, halo, 2))


def _layer_norm(z, g, b):
    mu = jnp.mean(z, axis=-1, keepdims=True)
    zc = z - mu
    var = jnp.mean(zc * zc, axis=-1, keepdims=True)
    return zc * lax.rsqrt(var + LN_EPS) * g + b


def _inproj_kernel(x_ref, w_ref, cos_ref, sin_ref, cw_ref, wu_ref, wd_ref,
                   q_ref, kd_ref, vd_ref, conv_ref, wu16_ref, wd16_ref, carry_ref,
                   *, tiles_per_seq):
    wu16_ref[...] = wu_ref[...].astype(BF16)
    wd16_ref[...] = wd_ref[...].astype(BF16)
    kv_end = ATTN_WIDTH + 2 * KV_WIDTH
    wq_ref = w_ref.at[:, :ATTN_WIDTH]
    wkv_ref = w_ref.at[:, ATTN_WIDTH:kv_end]
    wbch_ref = w_ref.at[:, kv_end:]
    i = pl.program_id(0)
    tm = x_ref.shape[0]
    sb = SB_IN
    nsub = tm // sb
    lane = lax.broadcasted_iota(jnp.int32, (sb, LANES), 1)
    first_half = (lane & (HEAD_DIM // 2)) == 0
    low_head = lane < HEAD_DIM
    scale = HEAD_DIM ** -0.5

    @pl.when(i % tiles_per_seq == 0)
    def _():
        carry_ref[...] = jnp.zeros_like(carry_ref)

    def dup_heads(t):
        r = pltpu.roll(t, HEAD_DIM, 1)
        return jnp.where(low_head, t, r), jnp.where(low_head, r, t)

    def project(r):
        xb = x_ref[r * sb:(r + 1) * sb, :].astype(BF16)
        return _dot(xb, wbch_ref[...]), _dot(xb, wq_ref[...]), _dot(xb, wkv_ref[...])

    halo = carry_ref[...]
    nxt = project(0)
    for r in range(nsub):
        rows = slice(r * sb, (r + 1) * sb)
        bch, q, kv = nxt
        if r + 1 < nsub:
            nxt = project(r + 1)
        cos = cos_ref[rows, :]
        sin = sin_ref[rows, :]

        def rope(t):
            partner = jnp.where(first_half, pltpu.roll(t, LANES - HEAD_DIM // 2, 1),
                                pltpu.roll(t, HEAD_DIM // 2, 1))
            return t * cos + partner * sin

        gate_b = bch[:, :CONV_WIDTH]
        u = bch[:, CONV_WIDTH:2 * CONV_WIDTH] * bch[:, 2 * CONV_WIDTH:]
        conv_ref[rows, :] = (gate_b * _causal_conv3(u, halo, cw_ref)).astype(BF16)
        halo = u[sb - SUBLANES:, :]
        for c in range(ATTN_WIDTH // LANES):
            sl = slice(c * LANES, (c + 1) * LANES)
            q_ref[rows, sl] = (rope(q[:, sl]) * scale).astype(BF16)
        k0, k1 = dup_heads(rope(kv[:, :KV_WIDTH]))
        v0, v1 = dup_heads(kv[:, KV_WIDTH:])
        kd_ref[rows, :LANES] = k0.astype(BF16)
        kd_ref[rows, LANES:] = k1.astype(BF16)
        vd_ref[rows, :LANES] = v0.astype(BF16)
        vd_ref[rows, LANES:] = v1.astype(BF16)
    carry_ref[...] = halo


def _inproj(x2, w_in, cos_t, sin_t, conv_w, w_up, w_down, seq):
    t = x2.shape[0]
    tm = TM_IN
    steps = t // tm
    tiles_per_seq = seq // tm
    bf16_rows = 2 * SUBLANES
    ru, rd = w_up.shape[0] // steps, w_down.shape[0] // steps
    assert ru * steps == w_up.shape[0] and rd * steps == w_down.shape[0]
    assert ru % bf16_rows == 0 and rd % bf16_rows == 0
    const = lambda i: (0, 0)
    row = lambda i: (i, 0)
    return pl.pallas_call(
        functools.partial(_inproj_kernel, tiles_per_seq=tiles_per_seq),
        grid=(steps,),
        in_specs=[
            pl.BlockSpec((tm, D_MODEL), row),
            pl.BlockSpec(w_in.shape, const, pipeline_mode=pl.Buffered(1)),
            pl.BlockSpec((tm, LANES), lambda i: (i % tiles_per_seq, 0)),
            pl.BlockSpec((tm, LANES), lambda i: (i % tiles_per_seq, 0)),
            pl.BlockSpec((CONV_K, CONV_WIDTH), const),
            pl.BlockSpec((ru, w_up.shape[1]), row),
            pl.BlockSpec((rd, w_down.shape[1]), row),
        ],
        out_specs=[
            pl.BlockSpec((tm, ATTN_WIDTH), row),
            pl.BlockSpec((tm, 2 * LANES), row),
            pl.BlockSpec((tm, 2 * LANES), row),
            pl.BlockSpec((tm, CONV_WIDTH), row),
            pl.BlockSpec((ru, w_up.shape[1]), row),
            pl.BlockSpec((rd, w_down.shape[1]), row),
        ],
        out_shape=[
            jax.ShapeDtypeStruct((t, ATTN_WIDTH), BF16),
            jax.ShapeDtypeStruct((t, 2 * LANES), BF16),
            jax.ShapeDtypeStruct((t, 2 * LANES), BF16),
            jax.ShapeDtypeStruct((t, CONV_WIDTH), BF16),
            jax.ShapeDtypeStruct(w_up.shape, BF16),
            jax.ShapeDtypeStruct(w_down.shape, BF16),
        ],
        scratch_shapes=[pltpu.VMEM((SUBLANES, CONV_WIDTH), F32)],
        compiler_params=pltpu.CompilerParams(
            dimension_semantics=("arbitrary",), vmem_limit_bytes=VMEM_LIMIT_BYTES),
        name="inproj",
    )(x2, w_in, cos_t, sin_t, conv_w, w_up, w_down)


def _attn_kernel(sink_ref, q_ref, kc_ref, kp_ref, vc_ref, vp_ref, o_ref, *, blocks_per_seq):
    i = pl.program_id(0)
    tq = q_ref.shape[0]
    nb = tq // BLOCK
    kfull = jnp.concatenate([kp_ref[...], kc_ref[...]], axis=0)
    vfull = jnp.concatenate([vp_ref[...], vc_ref[...]], axis=0)
    assert WINDOW == BLOCK and blocks_per_seq % nb == 0
    qi = lax.broadcasted_iota(jnp.int32, (BLOCK, BLOCK), 0)
    cj = lax.broadcasted_iota(jnp.int32, (BLOCK, BLOCK), 1)
    from_prev = cj > qi
    lane = lax.broadcasted_iota(jnp.int32, (BLOCK, LANES), 1)
    low_head = lane < HEAD_DIM
    zero = jnp.zeros((BLOCK, LANES), BF16)

    for b in range(nb):
        rows = slice(b * BLOCK, (b + 1) * BLOCK)
        band = slice(b * BLOCK, (b + 2) * BLOCK)
        no_prev = from_prev & ((i * nb) % blocks_per_seq == 0) if b == 0 else None
        for kvh in range(N_KV_HEADS):
            cols = slice(kvh * LANES, (kvh + 1) * LANES)
            k2 = kfull[band, cols]
            v2 = vfull[band, cols]
            qs = []
            for p in range(Q_PER_KV // 2):
                c0 = kvh * Q_PER_KV * HEAD_DIM + p * LANES
                qp = q_ref[rows, c0:c0 + LANES]
                qs.append(jnp.where(low_head, qp, zero))
                qs.append(jnp.where(low_head, zero, qp))
            s_all = lax.dot_general(jnp.concatenate(qs, axis=0), k2,
                                    (((1,), (1,)), ((), ())), preferred_element_type=F32)
            ps, inv_l = [], []
            for g in range(Q_PER_KV):
                sg = s_all[g * BLOCK:(g + 1) * BLOCK]
                s = jnp.where(from_prev, sg[:, :BLOCK], sg[:, BLOCK:])
                if no_prev is not None:
                    s = jnp.where(no_prev, NEG_INF, s)
                sink = sink_ref[kvh * Q_PER_KV + g]
                m = jnp.maximum(jnp.max(s, axis=-1, keepdims=True), sink)
                pr = jnp.exp(s - m)
                l = jnp.sum(pr, axis=-1, keepdims=True) + jnp.exp(sink - m)
                prb = pr.astype(BF16)
                ps.append(jnp.concatenate([jnp.where(from_prev, prb, zero),
                                           jnp.where(from_prev, zero, prb)], axis=1))
                inv_l.append(1.0 / l)
            o_all = _dot(jnp.concatenate(ps, axis=0), v2)
            for p in range(Q_PER_KV // 2):
                g0, g1 = 2 * p, 2 * p + 1
                o0 = o_all[g0 * BLOCK:(g0 + 1) * BLOCK] * inv_l[g0]
                o1 = o_all[g1 * BLOCK:(g1 + 1) * BLOCK] * inv_l[g1]
                c0 = kvh * Q_PER_KV * HEAD_DIM + p * LANES
                o_ref[rows, c0:c0 + LANES] = jnp.where(low_head, o0, o1).astype(BF16)


def _attn(sinks, q, kd, vd, seq):
    t = q.shape[0]
    tq = TQ_ATTN
    nb = tq // BLOCK
    row = lambda i: (i, 0)
    prev = lambda i: (jnp.maximum(i * nb - 1, 0), 0)
    return pl.pallas_call(
        functools.partial(_attn_kernel, blocks_per_seq=seq // BLOCK),
        grid=(t // tq,),
        in_specs=[
            pl.BlockSpec(memory_space=pltpu.SMEM),
            pl.BlockSpec((tq, ATTN_WIDTH), row),
            pl.BlockSpec((tq, 2 * LANES), row),
            pl.BlockSpec((BLOCK, 2 * LANES), prev),
            pl.BlockSpec((tq, 2 * LANES), row),
            pl.BlockSpec((BLOCK, 2 * LANES), prev),
        ],
        out_specs=pl.BlockSpec((tq, ATTN_WIDTH), row),
        out_shape=jax.ShapeDtypeStruct((t, ATTN_WIDTH), BF16),
        compiler_params=pltpu.CompilerParams(
            dimension_semantics=("arbitrary",), vmem_limit_bytes=VMEM_LIMIT_BYTES),
        name="attn",
    )(sinks, q, kd, kd, vd, vd)


def _outproj_kernel(x_ref, a_ref, c_ref, wa_ref, wc_ref, g_ref, b_ref, o_ref):
    sb = SB_OUT
    nsub = x_ref.shape[0] // sb

    def project(r):
        rows = slice(r * sb, (r + 1) * sb)
        return _dot(a_ref[rows, :], wa_ref[...]) + _dot(c_ref[rows, :], wc_ref[...])

    nxt = project(0)
    for r in range(nsub):
        rows = slice(r * sb, (r + 1) * sb)
        y = nxt
        if r + 1 < nsub:
            nxt = project(r + 1)
        z = DEEPNORM_ALPHA * x_ref[rows, :] + y
        o_ref[rows, :] = _layer_norm(z, g_ref[...], b_ref[...]).astype(o_ref.dtype)


def _outproj(x2, attn, conv, w_out, g, b):
    t = x2.shape[0]
    tm = TM_OUT
    row = lambda i: (i, 0)
    const = lambda i: (0, 0)
    return pl.pallas_call(
        _outproj_kernel,
        grid=(t // tm,),
        in_specs=[
            pl.BlockSpec((tm, D_MODEL), row),
            pl.BlockSpec((tm, ATTN_WIDTH), row),
            pl.BlockSpec((tm, CONV_WIDTH), row),
            pl.BlockSpec((ATTN_WIDTH, D_MODEL), const, pipeline_mode=pl.Buffered(1)),
            pl.BlockSpec((CONV_WIDTH, D_MODEL), lambda i: (1, 0), pipeline_mode=pl.Buffered(1)),
            pl.BlockSpec((1, D_MODEL), const),
            pl.BlockSpec((1, D_MODEL), const),
        ],
        out_specs=pl.BlockSpec((tm, D_MODEL), row),
        out_shape=jax.ShapeDtypeStruct((t, D_MODEL), BF16),
        compiler_params=pltpu.CompilerParams(
            dimension_semantics=("arbitrary",), vmem_limit_bytes=VMEM_LIMIT_BYTES),
        name="outproj",
    )(x2, attn, conv, w_out, w_out, g, b)


def _ffn_kernel(x_ref, wa_ref, wg_ref, cwa_ref, cwg_ref, wd_ref, g_ref, b_ref,
                o_ref, carry_a, carry_g, *, tiles_per_seq, first_width):
    i = pl.program_id(0)
    j = pl.program_id(1)
    tm = x_ref.shape[0]
    sb = SB_FFN
    nsub = tm // sb

    tf = wa_ref.shape[1]

    @pl.when(i % tiles_per_seq == 0)
    def _():
        carry_a[j] = jnp.zeros(carry_a.shape[1:], F32)
        carry_g[j] = jnp.zeros(carry_g.shape[1:], F32)

    def body(first, last):
        width = first_width if first else tf
        cols = slice(0, width)
        wa, wg, wd = wa_ref.at[:, cols], wg_ref.at[:, cols], wd_ref.at[cols, :]
        cwa, cwg = cwa_ref.at[:, cols], cwg_ref.at[:, cols]
        piece = SB_FFN_LAST if last else sb

        def up(r):
            xs = x_ref[r * sb:(r + 1) * sb, :]
            return _dot(xs, wa[...]), _dot(xs, wg[...])

        halo_a, halo_g = carry_a[j][:, :width], carry_g[j][:, :width]
        nxt = up(0)
        for r in range(nsub):
            ua, ug = nxt
            if r + 1 < nsub:
                nxt = up(r + 1)
            a = _causal_conv3(ua, halo_a, cwa)
            gate = _causal_conv3(ug, halo_g, cwg)
            halo_a, halo_g = ua[sb - SUBLANES:, :], ug[sb - SUBLANES:, :]
            h = (a * jax.nn.sigmoid(a) * gate).astype(BF16)
            for p in range(sb // piece):
                rows = slice(r * sb + p * piece, r * sb + (p + 1) * piece)
                if first:
                    acc = DEEPNORM_ALPHA * x_ref[rows, :].astype(F32)
                else:
                    acc = o_ref[rows, :]
                acc = acc + _dot(h[p * piece:(p + 1) * piece, :], wd[...])
                if last:
                    acc = _layer_norm(acc, g_ref[...], b_ref[...])
                o_ref[rows, :] = acc
        carry_a[j, :, :width] = halo_a
        carry_g[j, :, :width] = halo_g

    last_j = pl.num_programs(1) - 1
    pl.when(j == 0)(functools.partial(body, True, False))
    pl.when((j > 0) & (j < last_j))(functools.partial(body, False, False))
    pl.when(j == last_j)(functools.partial(body, False, True))


def _ffn(x1, w_up, conv_w, w_down, g, b, seq):
    t = x1.shape[0]
    tm, tf = TM_FFN, TF_FFN
    nf = pl.cdiv(D_FF, tf)
    assert nf >= 2
    first_width = D_FF - (nf - 1) * tf
    const = lambda i, j: (0, 0)
    assert tf % first_width == 0 and D_FF % first_width == 0 and first_width % LANES == 0
    ratio = tf // first_width
    start = lambda j: jnp.maximum(j * ratio - (ratio - 1), 0) * first_width
    start_g = lambda j: (D_FF // first_width + jnp.maximum(j * ratio - (ratio - 1), 0)) * first_width
    el = pl.Element
    return pl.pallas_call(
        functools.partial(_ffn_kernel, tiles_per_seq=seq // tm, first_width=first_width),
        grid=(t // tm, nf),
        in_specs=[
            pl.BlockSpec((tm, D_MODEL), lambda i, j: (i, 0)),
            pl.BlockSpec((el(D_MODEL), el(tf)), lambda i, j: (0, start(j))),
            pl.BlockSpec((el(D_MODEL), el(tf)), lambda i, j: (0, start_g(j))),
            pl.BlockSpec((el(CONV_K), el(tf)), lambda i, j: (0, start(j))),
            pl.BlockSpec((el(CONV_K), el(tf)), lambda i, j: (0, start_g(j))),
            pl.BlockSpec((el(tf), el(D_MODEL)), lambda i, j: (start(j), 0)),
            pl.BlockSpec((1, D_MODEL), const),
            pl.BlockSpec((1, D_MODEL), const),
        ],
        out_specs=pl.BlockSpec((tm, D_MODEL), lambda i, j: (i, 0)),
        out_shape=jax.ShapeDtypeStruct((t, D_MODEL), F32),
        scratch_shapes=[pltpu.VMEM((nf, SUBLANES, tf), F32),
                        pltpu.VMEM((nf, SUBLANES, tf), F32)],
        compiler_params=pltpu.CompilerParams(
            dimension_semantics=("arbitrary", "arbitrary"), vmem_limit_bytes=VMEM_LIMIT_BYTES),
        name="ffn",
    )(x1, w_up, w_up, conv_w, conv_w, w_down, g, b)


def _rope_tables(seq):
    half = HEAD_DIM // 2
    lane = jnp.arange(LANES)
    inv_freq = ROPE_THETA ** (-(lane % half).astype(F32) / half)
    ang = jnp.arange(seq).astype(F32)[:, None] * inv_freq[None, :]
    sign = jnp.where(lane % HEAD_DIM < half, -1.0, 1.0).astype(F32)
    return jnp.cos(ang), jnp.sin(ang) * sign[None, :]


def kernel(x, w_in, attn_sinks, short_conv_w, w_out, ln1_g, ln1_b,
           ffn_w_up, ffn_conv_w, ffn_w_down, ln2_g, ln2_b):
    b, s, d = x.shape
    assert d == D_MODEL and s % max(TM_IN, TQ_ATTN, TM_FFN) == 0
    cos_t, sin_t = _rope_tables(s)
    h = x.reshape(b * s, d)
    for l in range(w_in.shape[0]):
        q, kd, vd, conv, w_up16, w_down16 = _inproj(
            h, w_in[l].astype(BF16), cos_t, sin_t, short_conv_w[l], ffn_w_up[l], ffn_w_down[l], s)
        attn = _attn(attn_sinks[l], q, kd, vd, s)
        x1 = _outproj(h, attn, conv, w_out[l], ln1_g[l][None], ln1_b[l][None])
        h = _ffn(x1, w_up16, ffn_conv_w[l], w_down16, ln2_g[l][None], ln2_b[l][None], s)
    return h.reshape(b, s, d)
```

```python
import functools

import jax
import jax.numpy as jnp
from jax import lax
from jax.experimental import pallas as pl
from jax.experimental.pallas import tpu as pltpu

D_MODEL = 2048
HEAD_DIM = 64
N_Q_HEADS = 16
N_KV_HEADS = 2
Q_PER_KV = N_Q_HEADS // N_KV_HEADS
ATTN_WIDTH = N_Q_HEADS * HEAD_DIM
KV_WIDTH = N_KV_HEADS * HEAD_DIM
WINDOW = 128
BLOCK = 128
ROPE_THETA = 10000.0
CONV_WIDTH = D_MODEL - ATTN_WIDTH
CONV_K = 3
D_FF = 5632
LN_EPS = 1e-5
DEPTH = 1
DEEPNORM_ALPHA = (2 * DEPTH) ** 0.25
NEG_INF = -1e30

LANES = 128
SUBLANES = 8
VMEM_BYTES_V7X = 64 * 1024 * 1024
VMEM_LIMIT_BYTES = VMEM_BYTES_V7X - 2 * 1024 * 1024

TM_IN = 512
SB_IN = 256
TQ_ATTN = 512
TM_OUT = 1024
SB_OUT = 256
TM_FFN = 1024
TF_FFN = 1024
SB_FFN = 512
SB_FFN_LAST = 256

BF16 = jnp.bfloat16
F32 = jnp.float32


def _dot(a, b):
    return jnp.dot(a, b, preferred_element_type=F32)


def _shift_rows(u, halo, k):
    r = pltpu.roll(u, k, 0)
    hr = pltpu.roll(halo, k, 0)
    row = lax.broadcasted_iota(jnp.int32, hr.shape, 0)
    head = jnp.where(row < k, hr, r[:SUBLANES])
    return jnp.concatenate([head, r[SUBLANES:]], axis=0)


def _causal_conv3(u, halo, w_ref):
    return (w_ref[2:3, :] * u
            + w_ref[1:2, :] * _shift_rows(u, halo, 1)
            + w_ref[0:1, :] * _shift_rows(u, halo, 2))


def _layer_norm(z, g, b):
    mu = jnp.mean(z, axis=-1, keepdims=True)
    zc = z - mu
    var = jnp.mean(zc * zc, axis=-1, keepdims=True)
    return zc * lax.rsqrt(var + LN_EPS) * g + b


def _inproj_kernel(x_ref, w_ref, cos_ref, sin_ref, cw_ref, wu_ref, wd_ref,
                   q_ref, kd_ref, vd_ref, conv_ref, wu16_ref, wd16_ref, carry_ref,
                   *, tiles_per_seq):
    wu16_ref[...] = wu_ref[...].astype(BF16)
    wd16_ref[...] = wd_ref[...].astype(BF16)
    kv_end = ATTN_WIDTH + 2 * KV_WIDTH
    wq_ref = w_ref.at[:, :ATTN_WIDTH]
    wkv_ref = w_ref.at[:, ATTN_WIDTH:kv_end]
    wbch_ref = w_ref.at[:, kv_end:]
    i = pl.program_id(0)
    tm = x_ref.shape[0]
    sb = SB_IN
    nsub = tm // sb
    lane = lax.broadcasted_iota(jnp.int32, (sb, LANES), 1)
    first_half = (lane & (HEAD_DIM // 2)) == 0
    low_head = lane < HEAD_DIM
    scale = HEAD_DIM ** -0.5

    @pl.when(i % tiles_per_seq == 0)
    def _():
        carry_ref[...] = jnp.zeros_like(carry_ref)

    def dup_heads(t):
        r = pltpu.roll(t, HEAD_DIM, 1)
        return jnp.where(low_head, t, r), jnp.where(low_head, r, t)

    def project(r):
        xb = x_ref[r * sb:(r + 1) * sb, :].astype(BF16)
        return _dot(xb, wbch_ref[...]), _dot(xb, wq_ref[...]), _dot(xb, wkv_ref[...])

    halo = carry_ref[...]
    nxt = project(0)
    for r in range(nsub):
        rows = slice(r * sb, (r + 1) * sb)
        bch, q, kv = nxt
        if r + 1 < nsub:
            nxt = project(r + 1)
        cos = cos_ref[rows, :]
        sin = sin_ref[rows, :]

        def rope(t):
            partner = jnp.where(first_half, pltpu.roll(t, LANES - HEAD_DIM // 2, 1),
                                pltpu.roll(t, HEAD_DIM // 2, 1))
            return t * cos + partner * sin

        gate_b = bch[:, :CONV_WIDTH]
        u = bch[:, CONV_WIDTH:2 * CONV_WIDTH] * bch[:, 2 * CONV_WIDTH:]
        conv_ref[rows, :] = (gate_b * _causal_conv3(u, halo, cw_ref)).astype(BF16)
        halo = u[sb - SUBLANES:, :]
        for c in range(ATTN_WIDTH // LANES):
            sl = slice(c * LANES, (c + 1) * LANES)
            q_ref[rows, sl] = (rope(q[:, sl]) * scale).astype(BF16)
        k0, k1 = dup_heads(rope(kv[:, :KV_WIDTH]))
        v0, v1 = dup_heads(kv[:, KV_WIDTH:])
        kd_ref[rows, :LANES] = k0.astype(BF16)
        kd_ref[rows, LANES:] = k1.astype(BF16)
        vd_ref[rows, :LANES] = v0.astype(BF16)
        vd_ref[rows, LANES:] = v1.astype(BF16)
    carry_ref[...] = halo


def _inproj(x2, w_in, cos_t, sin_t, conv_w, w_up, w_down, seq):
    t = x2.shape[0]
    tm = TM_IN
    steps = t // tm
    tiles_per_seq = seq // tm
    bf16_rows = 2 * SUBLANES
    ru, rd = w_up.shape[0] // steps, w_down.shape[0] // steps
    assert ru * steps == w_up.shape[0] and rd * steps == w_down.shape[0]
    assert ru % bf16_rows == 0 and rd % bf16_rows == 0
    const = lambda i: (0, 0)
    row = lambda i: (i, 0)
    return pl.pallas_call(
        functools.partial(_inproj_kernel, tiles_per_seq=tiles_per_seq),
        grid=(steps,),
        in_specs=[
            pl.BlockSpec((tm, D_MODEL), row),
            pl.BlockSpec(w_in.shape, const, pipeline_mode=pl.Buffered(1)),
            pl.BlockSpec((tm, LANES), lambda i: (i % tiles_per_seq, 0)),
            pl.BlockSpec((tm, LANES), lambda i: (i % tiles_per_seq, 0)),
            pl.BlockSpec((CONV_K, CONV_WIDTH), const),
            pl.BlockSpec((ru, w_up.shape[1]), row),
            pl.BlockSpec((rd, w_down.shape[1]), row),
        ],
        out_specs=[
            pl.BlockSpec((tm, ATTN_WIDTH), row),
            pl.BlockSpec((tm, 2 * LANES), row),
            pl.BlockSpec((tm, 2 * LANES), row),
            pl.BlockSpec((tm, CONV_WIDTH), row),
            pl.BlockSpec((ru, w_up.shape[1]), row),
            pl.BlockSpec((rd, w_down.shape[1]), row),
        ],
        out_shape=[
            jax.ShapeDtypeStruct((t, ATTN_WIDTH), BF16),
            jax.ShapeDtypeStruct((t, 2 * LANES), BF16),
            jax.ShapeDtypeStruct((t, 2 * LANES), BF16),
            jax.ShapeDtypeStruct((t, CONV_WIDTH), BF16),
            jax.ShapeDtypeStruct(w_up.shape, BF16),
            jax.ShapeDtypeStruct(w_down.shape, BF16),
        ],
        scratch_shapes=[pltpu.VMEM((SUBLANES, CONV_WIDTH), F32)],
        compiler_params=pltpu.CompilerParams(
            dimension_semantics=("arbitrary",), vmem_limit_bytes=VMEM_LIMIT_BYTES),
        name="inproj",
    )(x2, w_in, cos_t, sin_t, conv_w, w_up, w_down)


def _attn_kernel(sink_ref, q_ref, kc_ref, kp_ref, vc_ref, vp_ref, o_ref, *, blocks_per_seq):
    i = pl.program_id(0)
    tq = q_ref.shape[0]
    nb = tq // BLOCK
    kfull = jnp.concatenate([kp_ref[...], kc_ref[...]], axis=0)
    vfull = jnp.concatenate([vp_ref[...], vc_ref[...]], axis=0)
    assert WINDOW == BLOCK and blocks_per_seq % nb == 0
    qi = lax.broadcasted_iota(jnp.int32, (BLOCK, BLOCK), 0)
    cj = lax.broadcasted_iota(jnp.int32, (BLOCK, BLOCK), 1)
    from_prev = cj > qi
    lane = lax.broadcasted_iota(jnp.int32, (BLOCK, LANES), 1)
    low_head = lane < HEAD_DIM
    zero = jnp.zeros((BLOCK, LANES), BF16)

    for b in range(nb):
        rows = slice(b * BLOCK, (b + 1) * BLOCK)
        band = slice(b * BLOCK, (b + 2) * BLOCK)
        no_prev = from_prev & ((i * nb) % blocks_per_seq == 0) if b == 0 else None
        for kvh in range(N_KV_HEADS):
            cols = slice(kvh * LANES, (kvh + 1) * LANES)
            k2 = kfull[band, cols]
            v2 = vfull[band, cols]
            qs = []
            for p in range(Q_PER_KV // 2):
                c0 = kvh * Q_PER_KV * HEAD_DIM + p * LANES
                qp = q_ref[rows, c0:c0 + LANES]
                qs.append(jnp.where(low_head, qp, zero))
                qs.append(jnp.where(low_head, zero, qp))
            s_all = lax.dot_general(jnp.concatenate(qs, axis=0), k2,
                                    (((1,), (1,)), ((), ())), preferred_element_type=F32)
            ps, inv_l = [], []
            for g in range(Q_PER_KV):
                sg = s_all[g * BLOCK:(g + 1) * BLOCK]
                s = jnp.where(from_prev, sg[:, :BLOCK], sg[:, BLOCK:])
                if no_prev is not None:
                    s = jnp.where(no_prev, NEG_INF, s)
                sink = sink_ref[kvh * Q_PER_KV + g]
                m = jnp.maximum(jnp.max(s, axis=-1, keepdims=True), sink)
                pr = jnp.exp(s - m)
                l = jnp.sum(pr, axis=-1, keepdims=True) + jnp.exp(sink - m)
                prb = pr.astype(BF16)
                ps.append(jnp.concatenate([jnp.where(from_prev, prb, zero),
                                           jnp.where(from_prev, zero, prb)], axis=1))
                inv_l.append(1.0 / l)
            o_all = _dot(jnp.concatenate(ps, axis=0), v2)
            for p in range(Q_PER_KV // 2):
                g0, g1 = 2 * p, 2 * p + 1
                o0 = o_all[g0 * BLOCK:(g0 + 1) * BLOCK] * inv_l[g0]
                o1 = o_all[g1 * BLOCK:(g1 + 1) * BLOCK] * inv_l[g1]
                c0 = kvh * Q_PER_KV * HEAD_DIM + p * LANES
                o_ref[rows, c0:c0 + LANES] = jnp.where(low_head, o0, o1).astype(BF16)


def _attn(sinks, q, kd, vd, seq):
    t = q.shape[0]
    tq = TQ_ATTN
    nb = tq // BLOCK
    row = lambda i: (i, 0)
    prev = lambda i: (jnp.maximum(i * nb - 1, 0), 0)
    return pl.pallas_call(
        functools.partial(_attn_kernel, blocks_per_seq=seq // BLOCK),
        grid=(t // tq,),
        in_specs=[
            pl.BlockSpec(memory_space=pltpu.SMEM),
            pl.BlockSpec((tq, ATTN_WIDTH), row),
            pl.BlockSpec((tq, 2 * LANES), row),
            pl.BlockSpec((BLOCK, 2 * LANES), prev),
            pl.BlockSpec((tq, 2 * LANES), row),
            pl.BlockSpec((BLOCK, 2 * LANES), prev),
        ],
        out_specs=pl.BlockSpec((tq, ATTN_WIDTH), row),
        out_shape=jax.ShapeDtypeStruct((t, ATTN_WIDTH), BF16),
        compiler_params=pltpu.CompilerParams(
            dimension_semantics=("arbitrary",), vmem_limit_bytes=VMEM_LIMIT_BYTES),
        name="attn",
    )(sinks, q, kd, kd, vd, vd)


def _outproj_kernel(x_ref, a_ref, c_ref, wa_ref, wc_ref, g_ref, b_ref, o_ref):
    sb = SB_OUT
    nsub = x_ref.shape[0] // sb

    def project(r):
        rows = slice(r * sb, (r + 1) * sb)
        return _dot(a_ref[rows, :], wa_ref[...]) + _dot(c_ref[rows, :], wc_ref[...])

    nxt = project(0)
    for r in range(nsub):
        rows = slice(r * sb, (r + 1) * sb)
        y = nxt
        if r + 1 < nsub:
            nxt = project(r + 1)
        z = DEEPNORM_ALPHA * x_ref[rows, :] + y
        o_ref[rows, :] = _layer_norm(z, g_ref[...], b_ref[...]).astype(o_ref.dtype)


def _outproj(x2, attn, conv, w_out, g, b):
    t = x2.shape[0]
    tm = TM_OUT
    row = lambda i: (i, 0)
    const = lambda i: (0, 0)
    return pl.pallas_call(
        _outproj_kernel,
        grid=(t // tm,),
        in_specs=[
            pl.BlockSpec((tm, D_MODEL), row),
            pl.BlockSpec((tm, ATTN_WIDTH), row),
            pl.BlockSpec((tm, CONV_WIDTH), row),
            pl.BlockSpec((ATTN_WIDTH, D_MODEL), const, pipeline_mode=pl.Buffered(1)),
            pl.BlockSpec((CONV_WIDTH, D_MODEL), lambda i: (1, 0), pipeline_mode=pl.Buffered(1)),
            pl.BlockSpec((1, D_MODEL), const),
            pl.BlockSpec((1, D_MODEL), const),
        ],
        out_specs=pl.BlockSpec((tm, D_MODEL), row),
        out_shape=jax.ShapeDtypeStruct((t, D_MODEL), BF16),
        compiler_params=pltpu.CompilerParams(
            dimension_semantics=("arbitrary",), vmem_limit_bytes=VMEM_LIMIT_BYTES),
        name="outproj",
    )(x2, attn, conv, w_out, w_out, g, b)


def _ffn_kernel(x_ref, wa_ref, wg_ref, cwa_ref, cwg_ref, wd_ref, g_ref, b_ref,
                o_ref, carry_a, carry_g, *, tiles_per_seq, first_width):
    i = pl.program_id(0)
    j = pl.program_id(1)
    tm = x_ref.shape[0]
    sb = SB_FFN
    nsub = tm // sb

    tf = wa_ref.shape[1]

    @pl.when(i % tiles_per_seq == 0)
    def _():
        carry_a[j] = jnp.zeros(carry_a.shape[1:], F32)
        carry_g[j] = jnp.zeros(carry_g.shape[1:], F32)

    def body(first, last):
        width = first_width if first else tf
        cols = slice(0, width)
        wa, wg, wd = wa_ref.at[:, cols], wg_ref.at[:, cols], wd_ref.at[cols, :]
        cwa, cwg = cwa_ref.at[:, cols], cwg_ref.at[:, cols]
        piece = SB_FFN_LAST if last else sb

        def up(r):
            xs = x_ref[r * sb:(r + 1) * sb, :]
            return _dot(xs, wa[...]), _dot(xs, wg[...])

        halo_a, halo_g = carry_a[j][:, :width], carry_g[j][:, :width]
        nxt = up(0)
        for r in range(nsub):
            ua, ug = nxt
            if r + 1 < nsub:
                nxt = up(r + 1)
            a = _causal_conv3(ua, halo_a, cwa)
            gate = _causal_conv3(ug, halo_g, cwg)
            halo_a, halo_g = ua[sb - SUBLANES:, :], ug[sb - SUBLANES:, :]
            h = a * jax.nn.sigmoid(a) * gate
            for p in range(sb // piece):
                rows = slice(r * sb + p * piece, r * sb + (p + 1) * piece)
                if first:
                    acc = DEEPNORM_ALPHA * x_ref[rows, :].astype(F32)
                else:
                    acc = o_ref[rows, :]
                acc = acc + _dot(h[p * piece:(p + 1) * piece, :], wd[...])
                if last:
                    acc = _layer_norm(acc, g_ref[...], b_ref[...])
                o_ref[rows, :] = acc
        carry_a[j, :, :width] = halo_a
        carry_g[j, :, :width] = halo_g

    last_j = pl.num_programs(1) - 1
    pl.when(j == 0)(functools.partial(body, True, False))
    pl.when((j > 0) & (j < last_j))(functools.partial(body, False, False))
    pl.when(j == last_j)(functools.partial(body, False, True))


def _ffn(x1, w_up, conv_w, w_down, g, b, seq):
    t = x1.shape[0]
    tm, tf = TM_FFN, TF_FFN
    nf = pl.cdiv(D_FF, tf)
    assert nf >= 2
    first_width = D_FF - (nf - 1) * tf
    const = lambda i, j: (0, 0)
    assert tf % first_width == 0 and D_FF % first_width == 0 and first_width % LANES == 0
    ratio = tf // first_width
    start = lambda j: jnp.maximum(j * ratio - (ratio - 1), 0) * first_width
    start_g = lambda j: (D_FF // first_width + jnp.maximum(j * ratio - (ratio - 1), 0)) * first_width
    el = pl.Element
    return pl.pallas_call(
        functools.partial(_ffn_kernel, tiles_per_seq=seq // tm, first_width=first_width),
        grid=(t // tm, nf),
        in_specs=[
            pl.BlockSpec((tm, D_MODEL), lambda i, j: (i, 0)),
            pl.BlockSpec((el(D_MODEL), el(tf)), lambda i, j: (0, start(j))),
            pl.BlockSpec((el(D_MODEL), el(tf)), lambda i, j: (0, start_g(j))),
            pl.BlockSpec((el(CONV_K), el(tf)), lambda i, j: (0, start(j))),
            pl.BlockSpec((el(CONV_K), el(tf)), lambda i, j: (0, start_g(j))),
            pl.BlockSpec((el(tf), el(D_MODEL)), lambda i, j: (start(j), 0)),
            pl.BlockSpec((1, D_MODEL), const),
            pl.BlockSpec((1, D_MODEL), const),
        ],
        out_specs=pl.BlockSpec((tm, D_MODEL), lambda i, j: (i, 0)),
        out_shape=jax.ShapeDtypeStruct((t, D_MODEL), F32),
        scratch_shapes=[pltpu.VMEM((nf, SUBLANES, tf), F32),
                        pltpu.VMEM((nf, SUBLANES, tf), F32)],
        compiler_params=pltpu.CompilerParams(
            dimension_semantics=("arbitrary", "arbitrary"), vmem_limit_bytes=VMEM_LIMIT_BYTES),
        name="ffn",
    )(x1, w_up, w_up, conv_w, conv_w, w_down, g, b)


def _rope_tables(seq):
    half = HEAD_DIM // 2
    inv_freq = ROPE_THETA ** (-jnp.arange(half, dtype=F32) / half)
    ang = jnp.arange(seq).astype(F32)[:, None] * inv_freq[None, :]
    cos = jnp.cos(ang)
    sin = jnp.sin(ang)
    reps = LANES // HEAD_DIM
    return (jnp.tile(jnp.concatenate([cos, cos], axis=-1), (1, reps)),
            jnp.tile(jnp.concatenate([-sin, sin], axis=-1), (1, reps)))


def kernel(x, w_in, attn_sinks, short_conv_w, w_out, ln1_g, ln1_b,
           ffn_w_up, ffn_conv_w, ffn_w_down, ln2_g, ln2_b):
    b, s, d = x.shape
    assert d == D_MODEL and s % max(TM_IN, TQ_ATTN, TM_OUT, TM_FFN) == 0
    cos_t, sin_t = _rope_tables(s)
    h = x.reshape(b * s, d)
    for l in range(w_in.shape[0]):
        q, kd, vd, conv, w_up16, w_down16 = _inproj(
            h, w_in[l].astype(BF16), cos_t, sin_t, short_conv_w[l], ffn_w_up[l], ffn_w_down[l], s)
        attn = _attn(attn_sinks[l], q, kd, vd, s)
        x1 = _outproj(h, attn, conv, w_out[l], ln1_g[l][None], ln1_b[l][None])
        h = _ffn(x1, w_up16, ffn_conv_w[l], w_down16, ln2_g[l][None], ln2_b[l][None], s)
    return h.reshape(b, s, d)
```

```python
import functools

import jax
import jax.numpy as jnp
from jax import lax
from jax.experimental import pallas as pl
from jax.experimental.pallas import tpu as pltpu

D_MODEL = 2048
HEAD_DIM = 64
N_Q_HEADS = 16
N_KV_HEADS = 2
Q_PER_KV = N_Q_HEADS // N_KV_HEADS
ATTN_WIDTH = N_Q_HEADS * HEAD_DIM
KV_WIDTH = N_KV_HEADS * HEAD_DIM
WINDOW = 128
BLOCK = 128
ROPE_THETA = 10000.0
CONV_WIDTH = D_MODEL - ATTN_WIDTH
CONV_K = 3
D_FF = 5632
LN_EPS = 1e-5
DEPTH = 1
DEEPNORM_ALPHA = (2 * DEPTH) ** 0.25
NEG_INF = -1e30

LANES = 128
SUBLANES = 8
VMEM_BYTES_V7X = 64 * 1024 * 1024
VMEM_LIMIT_BYTES = VMEM_BYTES_V7X - 2 * 1024 * 1024

TM_IN = 512
SB_IN = 256
TQ_ATTN = 512
TM_OUT = 1024
SB_OUT = 256
TM_FFN = 1024
TF_FFN = 1024
SB_FFN = 512
SB_FFN_LAST = 256

BF16 = jnp.bfloat16
F32 = jnp.float32


def _dot(a, b):
    return jnp.dot(a, b, preferred_element_type=F32)


def _shift_rows(u, halo, k):
    r = pltpu.roll(u, k, 0)
    hr = pltpu.roll(halo, k, 0)
    row = lax.broadcasted_iota(jnp.int32, hr.shape, 0)
    head = jnp.where(row < k, hr, r[:SUBLANES])
    return jnp.concatenate([head, r[SUBLANES:]], axis=0)


def _causal_conv3(u, halo, w_ref):
    return (w_ref[2:3, :] * u
            + w_ref[1:2, :] * _shift_rows(u, halo, 1)
            + w_ref[0:1, :] * _shift_rows(u, halo, 2))


def _layer_norm(z, g, b):
    mu = jnp.mean(z, axis=-1, keepdims=True)
    zc = z - mu
    var = jnp.mean(zc * zc, axis=-1, keepdims=True)
    return zc * lax.rsqrt(var + LN_EPS) * g + b


def _inproj_kernel(x_ref, w_ref, cos_ref, sin_ref, cw_ref, wu_ref, wd_ref,
                   q_ref, kd_ref, vd_ref, conv_ref, wu16_ref, wd16_ref, carry_ref,
                   *, tiles_per_seq):
    wu16_ref[...] = wu_ref[...].astype(BF16)
    wd16_ref[...] = wd_ref[...].astype(BF16)
    kv_end = ATTN_WIDTH + 2 * KV_WIDTH
    wq_ref = w_ref.at[:, :ATTN_WIDTH]
    wkv_ref = w_ref.at[:, ATTN_WIDTH:kv_end]
    wbch_ref = w_ref.at[:, kv_end:]
    i = pl.program_id(0)
    tm = x_ref.shape[0]
    sb = SB_IN
    nsub = tm // sb
    lane = lax.broadcasted_iota(jnp.int32, (sb, LANES), 1)
    first_half = (lane & (HEAD_DIM // 2)) == 0
    low_head = lane < HEAD_DIM
    scale = HEAD_DIM ** -0.5

    @pl.when(i % tiles_per_seq == 0)
    def _():
        carry_ref[...] = jnp.zeros_like(carry_ref)

    def dup_heads(t):
        r = pltpu.roll(t, HEAD_DIM, 1)
        return jnp.where(low_head, t, r), jnp.where(low_head, r, t)

    def project(r):
        xb = x_ref[r * sb:(r + 1) * sb, :]
        return _dot(xb, wbch_ref[...]), _dot(xb, wq_ref[...]), _dot(xb, wkv_ref[...])

    halo = carry_ref[...]
    nxt = project(0)
    for r in range(nsub):
        rows = slice(r * sb, (r + 1) * sb)
        bch, q, kv = nxt
        if r + 1 < nsub:
            nxt = project(r + 1)
        cos = cos_ref[rows, :]
        sin = sin_ref[rows, :]

        def rope(t):
            partner = jnp.where(first_half, pltpu.roll(t, LANES - HEAD_DIM // 2, 1),
                                pltpu.roll(t, HEAD_DIM // 2, 1))
            return t * cos + partner * sin

        gate_b = bch[:, :CONV_WIDTH]
        u = bch[:, CONV_WIDTH:2 * CONV_WIDTH] * bch[:, 2 * CONV_WIDTH:]
        conv_ref[rows, :] = (gate_b * _causal_conv3(u, halo, cw_ref)).astype(BF16)
        halo = u[sb - SUBLANES:, :]
        for c in range(ATTN_WIDTH // LANES):
            sl = slice(c * LANES, (c + 1) * LANES)
            q_ref[rows, sl] = (rope(q[:, sl]) * scale).astype(BF16)
        k0, k1 = dup_heads(rope(kv[:, :KV_WIDTH]))
        v0, v1 = dup_heads(kv[:, KV_WIDTH:])
        kd_ref[rows, :LANES] = k0.astype(BF16)
        kd_ref[rows, LANES:] = k1.astype(BF16)
        vd_ref[rows, :LANES] = v0.astype(BF16)
        vd_ref[rows, LANES:] = v1.astype(BF16)
    carry_ref[...] = halo


def _inproj(x2, w_in, cos_t, sin_t, conv_w, w_up, w_down, seq):
    t = x2.shape[0]
    tm = TM_IN
    steps = t // tm
    tiles_per_seq = seq // tm
    bf16_rows = 2 * SUBLANES
    ru, rd = w_up.shape[0] // steps, w_down.shape[0] // steps
    assert ru * steps == w_up.shape[0] and rd * steps == w_down.shape[0]
    assert ru % bf16_rows == 0 and rd % bf16_rows == 0
    const = lambda i: (0, 0)
    row = lambda i: (i, 0)
    return pl.pallas_call(
        functools.partial(_inproj_kernel, tiles_per_seq=tiles_per_seq),
        grid=(steps,),
        in_specs=[
            pl.BlockSpec((tm, D_MODEL), row),
            pl.BlockSpec(w_in.shape, const, pipeline_mode=pl.Buffered(1)),
            pl.BlockSpec((tm, LANES), lambda i: (i % tiles_per_seq, 0)),
            pl.BlockSpec((tm, LANES), lambda i: (i % tiles_per_seq, 0)),
            pl.BlockSpec((CONV_K, CONV_WIDTH), const),
            pl.BlockSpec((ru, w_up.shape[1]), row),
            pl.BlockSpec((rd, w_down.shape[1]), row),
        ],
        out_specs=[
            pl.BlockSpec((tm, ATTN_WIDTH), row),
            pl.BlockSpec((tm, 2 * LANES), row),
            pl.BlockSpec((tm, 2 * LANES), row),
            pl.BlockSpec((tm, CONV_WIDTH), row),
            pl.BlockSpec((ru, w_up.shape[1]), row),
            pl.BlockSpec((rd, w_down.shape[1]), row),
        ],
        out_shape=[
            jax.ShapeDtypeStruct((t, ATTN_WIDTH), BF16),
            jax.ShapeDtypeStruct((t, 2 * LANES), BF16),
            jax.ShapeDtypeStruct((t, 2 * LANES), BF16),
            jax.ShapeDtypeStruct((t, CONV_WIDTH), BF16),
            jax.ShapeDtypeStruct(w_up.shape, BF16),
            jax.ShapeDtypeStruct(w_down.shape, BF16),
        ],
        scratch_shapes=[pltpu.VMEM((SUBLANES, CONV_WIDTH), F32)],
        compiler_params=pltpu.CompilerParams(
            dimension_semantics=("arbitrary",), vmem_limit_bytes=VMEM_LIMIT_BYTES),
        name="inproj",
    )(x2, w_in, cos_t, sin_t, conv_w, w_up, w_down)


def _attn_kernel(sink_ref, q_ref, kc_ref, kp_ref, vc_ref, vp_ref, o_ref, *, blocks_per_seq):
    i = pl.program_id(0)
    tq = q_ref.shape[0]
    nb = tq // BLOCK
    kfull = jnp.concatenate([kp_ref[...], kc_ref[...]], axis=0)
    vfull = jnp.concatenate([vp_ref[...], vc_ref[...]], axis=0)
    assert WINDOW == BLOCK and blocks_per_seq % nb == 0
    qi = lax.broadcasted_iota(jnp.int32, (BLOCK, BLOCK), 0)
    cj = lax.broadcasted_iota(jnp.int32, (BLOCK, BLOCK), 1)
    from_prev = cj > qi
    lane = lax.broadcasted_iota(jnp.int32, (BLOCK, LANES), 1)
    low_head = lane < HEAD_DIM
    zero = jnp.zeros((BLOCK, LANES), BF16)

    for b in range(nb):
        rows = slice(b * BLOCK, (b + 1) * BLOCK)
        band = slice(b * BLOCK, (b + 2) * BLOCK)
        no_prev = from_prev & ((i * nb) % blocks_per_seq == 0) if b == 0 else None
        for kvh in range(N_KV_HEADS):
            cols = slice(kvh * LANES, (kvh + 1) * LANES)
            k2 = kfull[band, cols]
            v2 = vfull[band, cols]
            qs = []
            for p in range(Q_PER_KV // 2):
                c0 = kvh * Q_PER_KV * HEAD_DIM + p * LANES
                qp = q_ref[rows, c0:c0 + LANES]
                qs.append(jnp.where(low_head, qp, zero))
                qs.append(jnp.where(low_head, zero, qp))
            s_all = lax.dot_general(jnp.concatenate(qs, axis=0), k2,
                                    (((1,), (1,)), ((), ())), preferred_element_type=F32)
            ps, inv_l = [], []
            for g in range(Q_PER_KV):
                sg = s_all[g * BLOCK:(g + 1) * BLOCK]
                s = jnp.where(from_prev, sg[:, :BLOCK], sg[:, BLOCK:])
                if no_prev is not None:
                    s = jnp.where(no_prev, NEG_INF, s)
                sink = sink_ref[kvh * Q_PER_KV + g]
                m = jnp.maximum(jnp.max(s, axis=-1, keepdims=True), sink)
                pr = jnp.exp(s - m)
                l = jnp.sum(pr, axis=-1, keepdims=True) + jnp.exp(sink - m)
                prb = pr.astype(BF16)
                ps.append(jnp.concatenate([jnp.where(from_prev, prb, zero),
                                           jnp.where(from_prev, zero, prb)], axis=1))
                inv_l.append(1.0 / l)
            o_all = _dot(jnp.concatenate(ps, axis=0), v2)
            for p in range(Q_PER_KV // 2):
                g0, g1 = 2 * p, 2 * p + 1
                o0 = o_all[g0 * BLOCK:(g0 + 1) * BLOCK] * inv_l[g0]
                o1 = o_all[g1 * BLOCK:(g1 + 1) * BLOCK] * inv_l[g1]
                c0 = kvh * Q_PER_KV * HEAD_DIM + p * LANES
                o_ref[rows, c0:c0 + LANES] = jnp.where(low_head, o0, o1).astype(BF16)


def _attn(sinks, q, kd, vd, seq):
    t = q.shape[0]
    tq = TQ_ATTN
    nb = tq // BLOCK
    row = lambda i: (i, 0)
    prev = lambda i: (jnp.maximum(i * nb - 1, 0), 0)
    return pl.pallas_call(
        functools.partial(_attn_kernel, blocks_per_seq=seq // BLOCK),
        grid=(t // tq,),
        in_specs=[
            pl.BlockSpec(memory_space=pltpu.SMEM),
            pl.BlockSpec((tq, ATTN_WIDTH), row),
            pl.BlockSpec((tq, 2 * LANES), row),
            pl.BlockSpec((BLOCK, 2 * LANES), prev),
            pl.BlockSpec((tq, 2 * LANES), row),
            pl.BlockSpec((BLOCK, 2 * LANES), prev),
        ],
        out_specs=pl.BlockSpec((tq, ATTN_WIDTH), row),
        out_shape=jax.ShapeDtypeStruct((t, ATTN_WIDTH), BF16),
        compiler_params=pltpu.CompilerParams(
            dimension_semantics=("arbitrary",), vmem_limit_bytes=VMEM_LIMIT_BYTES),
        name="attn",
    )(sinks, q, kd, kd, vd, vd)


def _outproj_kernel(x_ref, a_ref, c_ref, wa_ref, wc_ref, g_ref, b_ref, o_ref):
    sb = SB_OUT
    nsub = x_ref.shape[0] // sb

    def project(r):
        rows = slice(r * sb, (r + 1) * sb)
        return _dot(a_ref[rows, :], wa_ref[...]) + _dot(c_ref[rows, :], wc_ref[...])

    nxt = project(0)
    for r in range(nsub):
        rows = slice(r * sb, (r + 1) * sb)
        y = nxt
        if r + 1 < nsub:
            nxt = project(r + 1)
        z = DEEPNORM_ALPHA * x_ref[rows, :] + y
        o_ref[rows, :] = _layer_norm(z, g_ref[...], b_ref[...]).astype(o_ref.dtype)


def _outproj(x2, attn, conv, w_out, g, b):
    t = x2.shape[0]
    tm = TM_OUT
    row = lambda i: (i, 0)
    const = lambda i: (0, 0)
    return pl.pallas_call(
        _outproj_kernel,
        grid=(t // tm,),
        in_specs=[
            pl.BlockSpec((tm, D_MODEL), row),
            pl.BlockSpec((tm, ATTN_WIDTH), row),
            pl.BlockSpec((tm, CONV_WIDTH), row),
            pl.BlockSpec((ATTN_WIDTH, D_MODEL), const, pipeline_mode=pl.Buffered(1)),
            pl.BlockSpec((CONV_WIDTH, D_MODEL), lambda i: (1, 0), pipeline_mode=pl.Buffered(1)),
            pl.BlockSpec((1, D_MODEL), const),
            pl.BlockSpec((1, D_MODEL), const),
        ],
        out_specs=pl.BlockSpec((tm, D_MODEL), row),
        out_shape=jax.ShapeDtypeStruct((t, D_MODEL), BF16),
        compiler_params=pltpu.CompilerParams(
            dimension_semantics=("arbitrary",), vmem_limit_bytes=VMEM_LIMIT_BYTES),
        name="outproj",
    )(x2, attn, conv, w_out, w_out, g, b)


def _ffn_kernel(x_ref, wa_ref, wg_ref, cwa_ref, cwg_ref, wd_ref, g_ref, b_ref,
                o_ref, carry_a, carry_g, *, tiles_per_seq, first_width):
    i = pl.program_id(0)
    j = pl.program_id(1)
    tm = x_ref.shape[0]
    sb = SB_FFN
    nsub = tm // sb
    tf = wd_ref.shape[0]

    @pl.when(i % tiles_per_seq == 0)
    def _():
        carry_a[j] = jnp.zeros(carry_a.shape[1:], F32)
        carry_g[j] = jnp.zeros(carry_g.shape[1:], F32)

    def body(first, last):
        width = first_width if first else tf
        cols = slice(0, width)
        wa, wg, wd = wa_ref.at[:, cols], wg_ref.at[:, cols], wd_ref.at[cols, :]
        cwa, cwg = cwa_ref.at[:, cols], cwg_ref.at[:, cols]
        piece = SB_FFN_LAST if last else sb

        def up(r):
            xs = x_ref[r * sb:(r + 1) * sb, :]
            return _dot(xs, wa[...]), _dot(xs, wg[...])

        halo_a, halo_g = carry_a[j][:, :width], carry_g[j][:, :width]
        nxt = up(0)
        for r in range(nsub):
            ua, ug = nxt
            if r + 1 < nsub:
                nxt = up(r + 1)
            a = _causal_conv3(ua, halo_a, cwa)
            gate = _causal_conv3(ug, halo_g, cwg)
            halo_a, halo_g = ua[sb - SUBLANES:, :], ug[sb - SUBLANES:, :]
            half_a = 0.5 * a
            h = (half_a + half_a * jnp.tanh(half_a)) * gate
            for p in range(sb // piece):
                rows = slice(r * sb + p * piece, r * sb + (p + 1) * piece)
                if first:
                    acc = DEEPNORM_ALPHA * x_ref[rows, :].astype(F32)
                else:
                    acc = o_ref[rows, :]
                acc = acc + _dot(h[p * piece:(p + 1) * piece, :], wd[...])
                if last:
                    acc = _layer_norm(acc, g_ref[...], b_ref[...])
                o_ref[rows, :] = acc
        carry_a[j, :, :width] = halo_a
        carry_g[j, :, :width] = halo_g

    last_j = pl.num_programs(1) - 1
    pl.when(j == 0)(functools.partial(body, True, False))
    pl.when((j > 0) & (j < last_j))(functools.partial(body, False, False))
    pl.when(j == last_j)(functools.partial(body, False, True))


def _ffn(x1, w_up, conv_w, w_down, g, b, seq):
    t = x1.shape[0]
    tm, tf = TM_FFN, TF_FFN
    nf = pl.cdiv(D_FF, tf)
    assert nf >= 2
    first_width = D_FF - (nf - 1) * tf
    const = lambda i, j: (0, 0)
    assert tf % first_width == 0 and D_FF % first_width == 0 and first_width % LANES == 0
    ratio = tf // first_width
    start = lambda j: jnp.maximum(j * ratio - (ratio - 1), 0) * first_width
    start_g = lambda j: (D_FF // first_width + jnp.maximum(j * ratio - (ratio - 1), 0)) * first_width
    el = pl.Element
    return pl.pallas_call(
        functools.partial(_ffn_kernel, tiles_per_seq=seq // tm, first_width=first_width),
        grid=(t // tm, nf),
        in_specs=[
            pl.BlockSpec((tm, D_MODEL), lambda i, j: (i, 0)),
            pl.BlockSpec((el(D_MODEL), el(tf)), lambda i, j: (0, start(j))),
            pl.BlockSpec((el(D_MODEL), el(tf)), lambda i, j: (0, start_g(j))),
            pl.BlockSpec((el(CONV_K), el(tf)), lambda i, j: (0, start(j))),
            pl.BlockSpec((el(CONV_K), el(tf)), lambda i, j: (0, start_g(j))),
            pl.BlockSpec((el(tf), el(D_MODEL)), lambda i, j: (start(j), 0)),
            pl.BlockSpec((1, D_MODEL), const),
            pl.BlockSpec((1, D_MODEL), const),
        ],
        out_specs=pl.BlockSpec((tm, D_MODEL), lambda i, j: (i, 0)),
        out_shape=jax.ShapeDtypeStruct((t, D_MODEL), F32),
        scratch_shapes=[pltpu.VMEM((nf, SUBLANES, tf), F32),
                        pltpu.VMEM((nf, SUBLANES, tf), F32)],
        compiler_params=pltpu.CompilerParams(
            dimension_semantics=("arbitrary", "arbitrary"), vmem_limit_bytes=VMEM_LIMIT_BYTES),
        name="ffn",
    )(x1, w_up, w_up, conv_w, conv_w, w_down, g, b)


def _rope_tables(seq):
    half = HEAD_DIM // 2
    inv_freq = ROPE_THETA ** (-jnp.arange(half, dtype=F32) / half)
    ang = jnp.arange(seq).astype(F32)[:, None] * inv_freq[None, :]
    cos = jnp.cos(ang)
    sin = jnp.sin(ang)
    reps = LANES // HEAD_DIM
    return (jnp.tile(jnp.concatenate([cos, cos], axis=-1), (1, reps)),
            jnp.tile(jnp.concatenate([-sin, sin], axis=-1), (1, reps)))


def kernel(x, w_in, attn_sinks, short_conv_w, w_out, ln1_g, ln1_b,
           ffn_w_up, ffn_conv_w, ffn_w_down, ln2_g, ln2_b):
    b, s, d = x.shape
    assert d == D_MODEL and s % max(TM_IN, TQ_ATTN, TM_OUT, TM_FFN) == 0
    cos_t, sin_t = _rope_tables(s)
    h = x.reshape(b * s, d)
    for l in range(w_in.shape[0]):
        q, kd, vd, conv, w_up16, w_down16 = _inproj(
            h, w_in[l].astype(BF16), cos_t, sin_t, short_conv_w[l], ffn_w_up[l], ffn_w_down[l], s)
        attn = _attn(attn_sinks[l], q, kd, vd, s)
        x1 = _outproj(h, attn, conv, w_out[l], ln1_g[l][None], ln1_b[l][None])
        h = _ffn(x1, w_up16, ffn_conv_w[l], w_down16, ln2_g[l][None], ln2_b[l][None], s)
    return h.reshape(b, s, d)
```

```python
import functools

import jax
import jax.numpy as jnp
from jax import lax
from jax.experimental import pallas as pl
from jax.experimental.pallas import tpu as pltpu

D_MODEL = 2048
HEAD_DIM = 64
N_Q_HEADS = 16
N_KV_HEADS = 2
Q_PER_KV = N_Q_HEADS // N_KV_HEADS
ATTN_WIDTH = N_Q_HEADS * HEAD_DIM
KV_WIDTH = N_KV_HEADS * HEAD_DIM
WINDOW = 128
BLOCK = 128
ROPE_THETA = 10000.0
CONV_WIDTH = D_MODEL - ATTN_WIDTH
CONV_K = 3
D_FF = 5632
LN_EPS = 1e-5
DEPTH = 1
DEEPNORM_ALPHA = (2 * DEPTH) ** 0.25
NEG_INF = -1e30

LANES = 128
SUBLANES = 8
VMEM_BYTES_V7X = 64 * 1024 * 1024
VMEM_LIMIT_BYTES = VMEM_BYTES_V7X - 2 * 1024 * 1024

TM_IN = 512
SB_IN = 256
TQ_ATTN = 2048
TM_OUT = 1024
SB_OUT = 256
TM_FFN = 1024
TF_FFN = 1024
SB_FFN = 512
SB_FFN_LAST = 256

BF16 = jnp.bfloat16
F32 = jnp.float32


def _dot(a, b):
    return jnp.dot(a, b, preferred_element_type=F32)


def _shift_rows(u, halo, k):
    r = pltpu.roll(u, k, 0)
    hr = pltpu.roll(halo, k, 0)
    row = lax.broadcasted_iota(jnp.int32, hr.shape, 0)
    head = jnp.where(row < k, hr, r[:SUBLANES])
    return jnp.concatenate([head, r[SUBLANES:]], axis=0)


def _causal_conv3(u, halo, w_ref):
    return (w_ref[2:3, :] * u
            + w_ref[1:2, :] * _shift_rows(u, halo, 1)
            + w_ref[0:1, :] * _shift_rows(u, halo, 2))


def _layer_norm(z, g, b):
    mu = jnp.mean(z, axis=-1, keepdims=True)
    zc = z - mu
    var = jnp.mean(zc * zc, axis=-1, keepdims=True)
    return zc * lax.rsqrt(var + LN_EPS) * g + b


def _inproj_kernel(x_ref, w_ref, cos_ref, sin_ref, cw_ref, wu_ref, wd_ref,
                   q_ref, kd_ref, vd_ref, conv_ref, wu16_ref, wd16_ref, carry_ref,
                   *, tiles_per_seq):
    wu16_ref[...] = wu_ref[...].astype(BF16)
    wd16_ref[...] = wd_ref[...].astype(BF16)
    kv_end = ATTN_WIDTH + 2 * KV_WIDTH
    wq_ref = w_ref.at[:, :ATTN_WIDTH]
    wkv_ref = w_ref.at[:, ATTN_WIDTH:kv_end]
    wbch_ref = w_ref.at[:, kv_end:]
    i = pl.program_id(0)
    tm = x_ref.shape[0]
    sb = SB_IN
    nsub = tm // sb
    lane = lax.broadcasted_iota(jnp.int32, (sb, LANES), 1)
    first_half = (lane & (HEAD_DIM // 2)) == 0
    low_head = lane < HEAD_DIM
    scale = HEAD_DIM ** -0.5

    @pl.when(i % tiles_per_seq == 0)
    def _():
        carry_ref[...] = jnp.zeros_like(carry_ref)

    def dup_heads(t):
        r = pltpu.roll(t, HEAD_DIM, 1)
        return jnp.where(low_head, t, r), jnp.where(low_head, r, t)

    def project(r):
        xb = x_ref[r * sb:(r + 1) * sb, :].astype(BF16)
        return _dot(xb, wbch_ref[...]), _dot(xb, wq_ref[...]), _dot(xb, wkv_ref[...])

    halo = carry_ref[...]
    nxt = project(0)
    for r in range(nsub):
        rows = slice(r * sb, (r + 1) * sb)
        bch, q, kv = nxt
        if r + 1 < nsub:
            nxt = project(r + 1)
        cos = cos_ref[rows, :]
        sin = sin_ref[rows, :]

        def rope(t):
            partner = jnp.where(first_half, pltpu.roll(t, LANES - HEAD_DIM // 2, 1),
                                pltpu.roll(t, HEAD_DIM // 2, 1))
            return t * cos + partner * sin

        gate_b = bch[:, :CONV_WIDTH]
        u = bch[:, CONV_WIDTH:2 * CONV_WIDTH] * bch[:, 2 * CONV_WIDTH:]
        conv_ref[rows, :] = (gate_b * _causal_conv3(u, halo, cw_ref)).astype(BF16)
        halo = u[sb - SUBLANES:, :]
        for c in range(ATTN_WIDTH // LANES):
            sl = slice(c * LANES, (c + 1) * LANES)
            q_ref[rows, sl] = (rope(q[:, sl]) * scale).astype(BF16)
        k0, k1 = dup_heads(rope(kv[:, :KV_WIDTH]))
        v0, v1 = dup_heads(kv[:, KV_WIDTH:])
        kd_ref[rows, :LANES] = k0.astype(BF16)
        kd_ref[rows, LANES:] = k1.astype(BF16)
        vd_ref[rows, :LANES] = v0.astype(BF16)
        vd_ref[rows, LANES:] = v1.astype(BF16)
    carry_ref[...] = halo


def _inproj(x2, w_in, cos_t, sin_t, conv_w, w_up, w_down, seq):
    t = x2.shape[0]
    tm = TM_IN
    steps = t // tm
    tiles_per_seq = seq // tm
    bf16_rows = 2 * SUBLANES
    ru, rd = w_up.shape[0] // steps, w_down.shape[0] // steps
    assert ru * steps == w_up.shape[0] and rd * steps == w_down.shape[0]
    assert ru % bf16_rows == 0 and rd % bf16_rows == 0
    const = lambda i: (0, 0)
    row = lambda i: (i, 0)
    return pl.pallas_call(
        functools.partial(_inproj_kernel, tiles_per_seq=tiles_per_seq),
        grid=(steps,),
        in_specs=[
            pl.BlockSpec((tm, D_MODEL), row),
            pl.BlockSpec(w_in.shape, const, pipeline_mode=pl.Buffered(1)),
            pl.BlockSpec((tm, LANES), lambda i: (i % tiles_per_seq, 0)),
            pl.BlockSpec((tm, LANES), lambda i: (i % tiles_per_seq, 0)),
            pl.BlockSpec((CONV_K, CONV_WIDTH), const),
            pl.BlockSpec((ru, w_up.shape[1]), row),
            pl.BlockSpec((rd, w_down.shape[1]), row),
        ],
        out_specs=[
            pl.BlockSpec((tm, ATTN_WIDTH), row),
            pl.BlockSpec((tm, 2 * LANES), row),
            pl.BlockSpec((tm, 2 * LANES), row),
            pl.BlockSpec((tm, CONV_WIDTH), row),
            pl.BlockSpec((ru, w_up.shape[1]), row),
            pl.BlockSpec((rd, w_down.shape[1]), row),
        ],
        out_shape=[
            jax.ShapeDtypeStruct((t, ATTN_WIDTH), BF16),
            jax.ShapeDtypeStruct((t, 2 * LANES), BF16),
            jax.ShapeDtypeStruct((t, 2 * LANES), BF16),
            jax.ShapeDtypeStruct((t, CONV_WIDTH), BF16),
            jax.ShapeDtypeStruct(w_up.shape, BF16),
            jax.ShapeDtypeStruct(w_down.shape, BF16),
        ],
        scratch_shapes=[pltpu.VMEM((SUBLANES, CONV_WIDTH), F32)],
        compiler_params=pltpu.CompilerParams(
            dimension_semantics=("arbitrary",), vmem_limit_bytes=VMEM_LIMIT_BYTES),
        name="inproj",
    )(x2, w_in, cos_t, sin_t, conv_w, w_up, w_down)


def _attn_kernel(sink_ref, q_ref, kc_ref, kp_ref, vc_ref, vp_ref, o_ref, *, blocks_per_seq):
    i = pl.program_id(0)
    tq = q_ref.shape[0]
    nb = tq // BLOCK
    kfull = jnp.concatenate([kp_ref[...], kc_ref[...]], axis=0)
    vfull = jnp.concatenate([vp_ref[...], vc_ref[...]], axis=0)
    assert WINDOW == BLOCK and blocks_per_seq % nb == 0
    qi = lax.broadcasted_iota(jnp.int32, (BLOCK, BLOCK), 0)
    cj = lax.broadcasted_iota(jnp.int32, (BLOCK, BLOCK), 1)
    from_prev = cj > qi
    lane = lax.broadcasted_iota(jnp.int32, (BLOCK, LANES), 1)
    low_head = lane < HEAD_DIM
    zero = jnp.zeros((BLOCK, LANES), BF16)

    for b in range(nb):
        rows = slice(b * BLOCK, (b + 1) * BLOCK)
        band = slice(b * BLOCK, (b + 2) * BLOCK)
        no_prev = from_prev & ((i * nb) % blocks_per_seq == 0) if b == 0 else None
        for kvh in range(N_KV_HEADS):
            cols = slice(kvh * LANES, (kvh + 1) * LANES)
            k2 = kfull[band, cols]
            v2 = vfull[band, cols]
            qs = []
            for p in range(Q_PER_KV // 2):
                c0 = kvh * Q_PER_KV * HEAD_DIM + p * LANES
                qp = q_ref[rows, c0:c0 + LANES]
                qs.append(jnp.where(low_head, qp, zero))
                qs.append(jnp.where(low_head, zero, qp))
            s_all = lax.dot_general(jnp.concatenate(qs, axis=0), k2,
                                    (((1,), (1,)), ((), ())), preferred_element_type=F32)
            ps, inv_l = [], []
            for g in range(Q_PER_KV):
                sg = s_all[g * BLOCK:(g + 1) * BLOCK]
                s = jnp.where(from_prev, sg[:, :BLOCK], sg[:, BLOCK:])
                if no_prev is not None:
                    s = jnp.where(no_prev, NEG_INF, s)
                sink = sink_ref[kvh * Q_PER_KV + g]
                m = jnp.maximum(jnp.max(s, axis=-1, keepdims=True), sink)
                pr = jnp.exp(s - m)
                l = jnp.sum(pr, axis=-1, keepdims=True) + jnp.exp(sink - m)
                prb = pr.astype(BF16)
                ps.append(jnp.concatenate([jnp.where(from_prev, prb, zero),
                                           jnp.where(from_prev, zero, prb)], axis=1))
                inv_l.append(1.0 / l)
            o_all = _dot(jnp.concatenate(ps, axis=0), v2)
            for p in range(Q_PER_KV // 2):
                g0, g1 = 2 * p, 2 * p + 1
                o0 = o_all[g0 * BLOCK:(g0 + 1) * BLOCK] * inv_l[g0]
                o1 = o_all[g1 * BLOCK:(g1 + 1) * BLOCK] * inv_l[g1]
                c0 = kvh * Q_PER_KV * HEAD_DIM + p * LANES
                o_ref[rows, c0:c0 + LANES] = jnp.where(low_head, o0, o1).astype(BF16)


def _attn(sinks, q, kd, vd, seq):
    t = q.shape[0]
    tq = TQ_ATTN
    nb = tq // BLOCK
    row = lambda i: (i, 0)
    prev = lambda i: (jnp.maximum(i * nb - 1, 0), 0)
    return pl.pallas_call(
        functools.partial(_attn_kernel, blocks_per_seq=seq // BLOCK),
        grid=(t // tq,),
        in_specs=[
            pl.BlockSpec(memory_space=pltpu.SMEM),
            pl.BlockSpec((tq, ATTN_WIDTH), row),
            pl.BlockSpec((tq, 2 * LANES), row),
            pl.BlockSpec((BLOCK, 2 * LANES), prev),
            pl.BlockSpec((tq, 2 * LANES), row),
            pl.BlockSpec((BLOCK, 2 * LANES), prev),
        ],
        out_specs=pl.BlockSpec((tq, ATTN_WIDTH), row),
        out_shape=jax.ShapeDtypeStruct((t, ATTN_WIDTH), BF16),
        compiler_params=pltpu.CompilerParams(
            dimension_semantics=("arbitrary",), vmem_limit_bytes=VMEM_LIMIT_BYTES),
        name="attn",
    )(sinks, q, kd, kd, vd, vd)


def _outproj_kernel(x_ref, a_ref, c_ref, wa_ref, wc_ref, g_ref, b_ref, o_ref):
    sb = SB_OUT
    nsub = x_ref.shape[0] // sb

    def project(r):
        rows = slice(r * sb, (r + 1) * sb)
        return _dot(a_ref[rows, :], wa_ref[...]) + _dot(c_ref[rows, :], wc_ref[...])

    nxt = project(0)
    for r in range(nsub):
        rows = slice(r * sb, (r + 1) * sb)
        y = nxt
        if r + 1 < nsub:
            nxt = project(r + 1)
        z = DEEPNORM_ALPHA * x_ref[rows, :] + y
        o_ref[rows, :] = _layer_norm(z, g_ref[...], b_ref[...]).astype(o_ref.dtype)


def _outproj(x2, attn, conv, w_out, g, b):
    t = x2.shape[0]
    tm = TM_OUT
    row = lambda i: (i, 0)
    const = lambda i: (0, 0)
    return pl.pallas_call(
        _outproj_kernel,
        grid=(t // tm,),
        in_specs=[
            pl.BlockSpec((tm, D_MODEL), row),
            pl.BlockSpec((tm, ATTN_WIDTH), row),
            pl.BlockSpec((tm, CONV_WIDTH), row),
            pl.BlockSpec((ATTN_WIDTH, D_MODEL), const, pipeline_mode=pl.Buffered(1)),
            pl.BlockSpec((CONV_WIDTH, D_MODEL), lambda i: (1, 0), pipeline_mode=pl.Buffered(1)),
            pl.BlockSpec((1, D_MODEL), const),
            pl.BlockSpec((1, D_MODEL), const),
        ],
        out_specs=pl.BlockSpec((tm, D_MODEL), row),
        out_shape=jax.ShapeDtypeStruct((t, D_MODEL), BF16),
        compiler_params=pltpu.CompilerParams(
            dimension_semantics=("arbitrary",), vmem_limit_bytes=VMEM_LIMIT_BYTES),
        name="outproj",
    )(x2, attn, conv, w_out, w_out, g, b)


def _ffn_kernel(x_ref, wa_ref, wg_ref, cwa_ref, cwg_ref, wd_ref, g_ref, b_ref,
                o_ref, carry_a, carry_g, *, tiles_per_seq, first_width):
    i = pl.program_id(0)
    j = pl.program_id(1)
    tm = x_ref.shape[0]
    sb = SB_FFN
    nsub = tm // sb

    tf = wa_ref.shape[1]

    @pl.when(i % tiles_per_seq == 0)
    def _():
        carry_a[j] = jnp.zeros(carry_a.shape[1:], F32)
        carry_g[j] = jnp.zeros(carry_g.shape[1:], F32)

    def body(first, last):
        width = first_width if first else tf
        cols = slice(0, width)
        wa, wg, wd = wa_ref.at[:, cols], wg_ref.at[:, cols], wd_ref.at[cols, :]
        cwa, cwg = cwa_ref.at[:, cols], cwg_ref.at[:, cols]
        piece = SB_FFN_LAST if last else sb

        def up(r):
            xs = x_ref[r * sb:(r + 1) * sb, :]
            return _dot(xs, wa[...]), _dot(xs, wg[...])

        halo_a, halo_g = carry_a[j][:, :width], carry_g[j][:, :width]
        nxt = up(0)
        for r in range(nsub):
            ua, ug = nxt
            if r + 1 < nsub:
                nxt = up(r + 1)
            a = _causal_conv3(ua, halo_a, cwa)
            gate = _causal_conv3(ug, halo_g, cwg)
            halo_a, halo_g = ua[sb - SUBLANES:, :], ug[sb - SUBLANES:, :]
            h = a * jax.nn.sigmoid(a) * gate
            for p in range(sb // piece):
                rows = slice(r * sb + p * piece, r * sb + (p + 1) * piece)
                if first:
                    acc = DEEPNORM_ALPHA * x_ref[rows, :].astype(F32)
                else:
                    acc = o_ref[rows, :]
                acc = acc + _dot(h[p * piece:(p + 1) * piece, :], wd[...])
                if last:
                    acc = _layer_norm(acc, g_ref[...], b_ref[...])
                o_ref[rows, :] = acc
        carry_a[j, :, :width] = halo_a
        carry_g[j, :, :width] = halo_g

    last_j = pl.num_programs(1) - 1
    pl.when(j == 0)(functools.partial(body, True, False))
    pl.when((j > 0) & (j < last_j))(functools.partial(body, False, False))
    pl.when(j == last_j)(functools.partial(body, False, True))


def _ffn(x1, w_up, conv_w, w_down, g, b, seq):
    t = x1.shape[0]
    tm, tf = TM_FFN, TF_FFN
    nf = pl.cdiv(D_FF, tf)
    assert nf >= 2
    first_width = D_FF - (nf - 1) * tf
    const = lambda i, j: (0, 0)
    assert tf % first_width == 0 and D_FF % first_width == 0 and first_width % LANES == 0
    ratio = tf // first_width
    start = lambda j: jnp.maximum(j * ratio - (ratio - 1), 0) * first_width
    start_g = lambda j: (D_FF // first_width + jnp.maximum(j * ratio - (ratio - 1), 0)) * first_width
    el = pl.Element
    return pl.pallas_call(
        functools.partial(_ffn_kernel, tiles_per_seq=seq // tm, first_width=first_width),
        grid=(t // tm, nf),
        in_specs=[
            pl.BlockSpec((tm, D_MODEL), lambda i, j: (i, 0)),
            pl.BlockSpec((el(D_MODEL), el(tf)), lambda i, j: (0, start(j))),
            pl.BlockSpec((el(D_MODEL), el(tf)), lambda i, j: (0, start_g(j))),
            pl.BlockSpec((el(CONV_K), el(tf)), lambda i, j: (0, start(j))),
            pl.BlockSpec((el(CONV_K), el(tf)), lambda i, j: (0, start_g(j))),
            pl.BlockSpec((el(tf), el(D_MODEL)), lambda i, j: (start(j), 0)),
            pl.BlockSpec((1, D_MODEL), const),
            pl.BlockSpec((1, D_MODEL), const),
        ],
        out_specs=pl.BlockSpec((tm, D_MODEL), lambda i, j: (i, 0)),
        out_shape=jax.ShapeDtypeStruct((t, D_MODEL), F32),
        scratch_shapes=[pltpu.VMEM((nf, SUBLANES, tf), F32),
                        pltpu.VMEM((nf, SUBLANES, tf), F32)],
        compiler_params=pltpu.CompilerParams(
            dimension_semantics=("arbitrary", "arbitrary"), vmem_limit_bytes=VMEM_LIMIT_BYTES),
        name="ffn",
    )(x1, w_up, w_up, conv_w, conv_w, w_down, g, b)


def _rope_tables(seq):
    half = HEAD_DIM // 2
    inv_freq = ROPE_THETA ** (-jnp.arange(half, dtype=F32) / half)
    ang = jnp.arange(seq).astype(F32)[:, None] * inv_freq[None, :]
    cos = jnp.cos(ang)
    sin = jnp.sin(ang)
    reps = LANES // HEAD_DIM
    return (jnp.tile(jnp.concatenate([cos, cos], axis=-1), (1, reps)),
            jnp.tile(jnp.concatenate([-sin, sin], axis=-1), (1, reps)))


def kernel(x, w_in, attn_sinks, short_conv_w, w_out, ln1_g, ln1_b,
           ffn_w_up, ffn_conv_w, ffn_w_down, ln2_g, ln2_b):
    b, s, d = x.shape
    assert d == D_MODEL and s % max(TM_IN, TQ_ATTN, TM_OUT, TM_FFN) == 0
    cos_t, sin_t = _rope_tables(s)
    h = x.reshape(b * s, d)
    for l in range(w_in.shape[0]):
        q, kd, vd, conv, w_up16, w_down16 = _inproj(
            h, w_in[l].astype(BF16), cos_t, sin_t, short_conv_w[l], ffn_w_up[l], ffn_w_down[l], s)
        attn = _attn(attn_sinks[l], q, kd, vd, s)
        x1 = _outproj(h, attn, conv, w_out[l], ln1_g[l][None], ln1_b[l][None])
        h = _ffn(x1, w_up16, ffn_conv_w[l], w_down16, ln2_g[l][None], ln2_b[l][None], s)
    return h.reshape(b, s, d)
```

```python
import functools

import jax
import jax.numpy as jnp
from jax import lax
from jax.experimental import pallas as pl
from jax.experimental.pallas import tpu as pltpu

D_MODEL = 2048
HEAD_DIM = 64
N_Q_HEADS = 16
N_KV_HEADS = 2
Q_PER_KV = N_Q_HEADS // N_KV_HEADS
ATTN_WIDTH = N_Q_HEADS * HEAD_DIM
KV_WIDTH = N_KV_HEADS * HEAD_DIM
WINDOW = 128
BLOCK = 128
ROPE_THETA = 10000.0
CONV_WIDTH = D_MODEL - ATTN_WIDTH
CONV_K = 3
D_FF = 5632
LN_EPS = 1e-5
DEPTH = 1
DEEPNORM_ALPHA = (2 * DEPTH) ** 0.25
NEG_INF = -1e30

LANES = 128
SUBLANES = 8
VMEM_BYTES_V7X = 64 * 1024 * 1024
VMEM_LIMIT_BYTES = VMEM_BYTES_V7X - 2 * 1024 * 1024

TM_IN = 512
SB_IN = 256
TQ_ATTN = 2048
TM_OUT = 1024
SB_OUT = 256
TM_FFN = 1024
TF_FFN = 1024
SB_FFN = 512
SB_FFN_LAST = 256

BF16 = jnp.bfloat16
F32 = jnp.float32


def _dot(a, b):
    return jnp.dot(a, b, preferred_element_type=F32)


def _shift_rows(u, halo, k):
    r = pltpu.roll(u, k, 0)
    hr = pltpu.roll(halo, k, 0)
    row = lax.broadcasted_iota(jnp.int32, hr.shape, 0)
    head = jnp.where(row < k, hr, r[:SUBLANES])
    return jnp.concatenate([head, r[SUBLANES:]], axis=0)


def _causal_conv3(u, halo, w_ref):
    return (w_ref[2:3, :] * u
            + w_ref[1:2, :] * _shift_rows(u, halo, 1)
            + w_ref[0:1, :] * _shift_rows(u, halo, 2))


def _layer_norm(z, g, b):
    mu = jnp.mean(z, axis=-1, keepdims=True)
    zc = z - mu
    var = jnp.mean(zc * zc, axis=-1, keepdims=True)
    return zc * lax.rsqrt(var + LN_EPS) * g + b


def _inproj_kernel(x_ref, w_ref, cos_ref, sin_ref, cw_ref, wu_ref, wd_ref,
                   q_ref, kd_ref, vd_ref, conv_ref, wu16_ref, wd16_ref, carry_ref,
                   *, tiles_per_seq):
    kv_end = ATTN_WIDTH + 2 * KV_WIDTH
    wq_ref = w_ref.at[:, :ATTN_WIDTH]
    wkv_ref = w_ref.at[:, ATTN_WIDTH:kv_end]
    wbch_ref = w_ref.at[:, kv_end:]
    i = pl.program_id(0)
    tm = x_ref.shape[0]
    sb = SB_IN
    nsub = tm // sb
    lane = lax.broadcasted_iota(jnp.int32, (sb, LANES), 1)
    first_half = (lane & (HEAD_DIM // 2)) == 0
    low_head = lane < HEAD_DIM
    scale = HEAD_DIM ** -0.5

    @pl.when(i % tiles_per_seq == 0)
    def _():
        carry_ref[...] = jnp.zeros_like(carry_ref)

    wu16_ref[...] = wu_ref[...].astype(BF16)
    wd16_ref[...] = wd_ref[...].astype(BF16)

    def dup_heads(t):
        r = pltpu.roll(t, HEAD_DIM, 1)
        return jnp.where(low_head, t, r), jnp.where(low_head, r, t)

    def project(r):
        xb = x_ref[r * sb:(r + 1) * sb, :].astype(BF16)
        q = _dot(xb, wq_ref[...])
        bch = _dot(xb, wbch_ref[...])
        return bch, q, _dot(xb, wkv_ref[...])

    halo = carry_ref[...]
    nxt = project(0)
    for r in range(nsub):
        rows = slice(r * sb, (r + 1) * sb)
        bch, q, kv = nxt
        if r + 1 < nsub:
            nxt = project(r + 1)
        cos = cos_ref[rows, :]
        sin = sin_ref[rows, :]

        def rope(t):
            partner = jnp.where(first_half, pltpu.roll(t, LANES - HEAD_DIM // 2, 1),
                                pltpu.roll(t, HEAD_DIM // 2, 1))
            return t * cos + partner * sin

        gate_b = bch[:, :CONV_WIDTH]
        u = bch[:, CONV_WIDTH:2 * CONV_WIDTH] * bch[:, 2 * CONV_WIDTH:]
        conv_ref[rows, :] = (gate_b * _causal_conv3(u, halo, cw_ref)).astype(BF16)
        halo = u[sb - SUBLANES:, :]
        for c in range(ATTN_WIDTH // LANES):
            sl = slice(c * LANES, (c + 1) * LANES)
            q_ref[rows, sl] = (rope(q[:, sl]) * scale).astype(BF16)
        k0, k1 = dup_heads(rope(kv[:, :KV_WIDTH]))
        v0, v1 = dup_heads(kv[:, KV_WIDTH:])
        kd_ref[rows, :LANES] = k0.astype(BF16)
        kd_ref[rows, LANES:] = k1.astype(BF16)
        vd_ref[rows, :LANES] = v0.astype(BF16)
        vd_ref[rows, LANES:] = v1.astype(BF16)
    carry_ref[...] = halo


def _inproj(x2, w_in, cos_t, sin_t, conv_w, w_up, w_down, seq):
    t = x2.shape[0]
    tm = TM_IN
    steps = t // tm
    tiles_per_seq = seq // tm
    bf16_rows = 2 * SUBLANES
    ru, rd = w_up.shape[0] // steps, w_down.shape[0] // steps
    assert ru * steps == w_up.shape[0] and rd * steps == w_down.shape[0]
    assert ru % bf16_rows == 0 and rd % bf16_rows == 0
    const = lambda i: (0, 0)
    row = lambda i: (i, 0)
    return pl.pallas_call(
        functools.partial(_inproj_kernel, tiles_per_seq=tiles_per_seq),
        grid=(steps,),
        in_specs=[
            pl.BlockSpec((tm, D_MODEL), row),
            pl.BlockSpec(w_in.shape, const, pipeline_mode=pl.Buffered(1)),
            pl.BlockSpec((tm, LANES), lambda i: (i % tiles_per_seq, 0)),
            pl.BlockSpec((tm, LANES), lambda i: (i % tiles_per_seq, 0)),
            pl.BlockSpec((CONV_K, CONV_WIDTH), const),
            pl.BlockSpec((ru, w_up.shape[1]), row),
            pl.BlockSpec((rd, w_down.shape[1]), row),
        ],
        out_specs=[
            pl.BlockSpec((tm, ATTN_WIDTH), row),
            pl.BlockSpec((tm, 2 * LANES), row),
            pl.BlockSpec((tm, 2 * LANES), row),
            pl.BlockSpec((tm, CONV_WIDTH), row),
            pl.BlockSpec((ru, w_up.shape[1]), row),
            pl.BlockSpec((rd, w_down.shape[1]), row),
        ],
        out_shape=[
            jax.ShapeDtypeStruct((t, ATTN_WIDTH), BF16),
            jax.ShapeDtypeStruct((t, 2 * LANES), BF16),
            jax.ShapeDtypeStruct((t, 2 * LANES), BF16),
            jax.ShapeDtypeStruct((t, CONV_WIDTH), BF16),
            jax.ShapeDtypeStruct(w_up.shape, BF16),
            jax.ShapeDtypeStruct(w_down.shape, BF16),
        ],
        scratch_shapes=[pltpu.VMEM((SUBLANES, CONV_WIDTH), F32)],
        compiler_params=pltpu.CompilerParams(
            dimension_semantics=("arbitrary",), vmem_limit_bytes=VMEM_LIMIT_BYTES),
        name="inproj",
    )(x2, w_in, cos_t, sin_t, conv_w, w_up, w_down)


def _attn_kernel(sink_ref, q_ref, kc_ref, kp_ref, vc_ref, vp_ref, o_ref, *, blocks_per_seq):
    i = pl.program_id(0)
    tq = q_ref.shape[0]
    nb = tq // BLOCK
    kfull = jnp.concatenate([kp_ref[...], kc_ref[...]], axis=0)
    vfull = jnp.concatenate([vp_ref[...], vc_ref[...]], axis=0)
    assert WINDOW == BLOCK and blocks_per_seq % nb == 0
    qi = lax.broadcasted_iota(jnp.int32, (BLOCK, BLOCK), 0)
    cj = lax.broadcasted_iota(jnp.int32, (BLOCK, BLOCK), 1)
    from_prev = cj > qi
    lane = lax.broadcasted_iota(jnp.int32, (BLOCK, LANES), 1)
    low_head = lane < HEAD_DIM
    zero = jnp.zeros((BLOCK, LANES), BF16)

    for b in range(nb):
        rows = slice(b * BLOCK, (b + 1) * BLOCK)
        band = slice(b * BLOCK, (b + 2) * BLOCK)
        no_prev = from_prev & ((i * nb) % blocks_per_seq == 0) if b == 0 else None
        for kvh in range(N_KV_HEADS):
            cols = slice(kvh * LANES, (kvh + 1) * LANES)
            k2 = kfull[band, cols]
            v2 = vfull[band, cols]
            qs = []
            for p in range(Q_PER_KV // 2):
                c0 = kvh * Q_PER_KV * HEAD_DIM + p * LANES
                qp = q_ref[rows, c0:c0 + LANES]
                qs.append(jnp.where(low_head, qp, zero))
                qs.append(jnp.where(low_head, zero, qp))
            s_all = lax.dot_general(jnp.concatenate(qs, axis=0), k2,
                                    (((1,), (1,)), ((), ())), preferred_element_type=F32)
            ps, inv_l = [], []
            for g in range(Q_PER_KV):
                sg = s_all[g * BLOCK:(g + 1) * BLOCK]
                s = jnp.where(from_prev, sg[:, :BLOCK], sg[:, BLOCK:])
                if no_prev is not None:
                    s = jnp.where(no_prev, NEG_INF, s)
                sink = sink_ref[kvh * Q_PER_KV + g]
                m = jnp.maximum(jnp.max(s, axis=-1, keepdims=True), sink)
                pr = jnp.exp(s - m)
                l = jnp.sum(pr, axis=-1, keepdims=True) + jnp.exp(sink - m)
                prb = pr.astype(BF16)
                ps.append(jnp.concatenate([jnp.where(from_prev, prb, zero),
                                           jnp.where(from_prev, zero, prb)], axis=1))
                inv_l.append(1.0 / l)
            o_all = _dot(jnp.concatenate(ps, axis=0), v2)
            for p in range(Q_PER_KV // 2):
                g0, g1 = 2 * p, 2 * p + 1
                o0 = o_all[g0 * BLOCK:(g0 + 1) * BLOCK] * inv_l[g0]
                o1 = o_all[g1 * BLOCK:(g1 + 1) * BLOCK] * inv_l[g1]
                c0 = kvh * Q_PER_KV * HEAD_DIM + p * LANES
                o_ref[rows, c0:c0 + LANES] = jnp.where(low_head, o0, o1).astype(BF16)


def _attn(sinks, q, kd, vd, seq):
    t = q.shape[0]
    tq = TQ_ATTN
    nb = tq // BLOCK
    row = lambda i: (i, 0)
    prev = lambda i: (jnp.maximum(i * nb - 1, 0), 0)
    return pl.pallas_call(
        functools.partial(_attn_kernel, blocks_per_seq=seq // BLOCK),
        grid=(t // tq,),
        in_specs=[
            pl.BlockSpec(memory_space=pltpu.SMEM),
            pl.BlockSpec((tq, ATTN_WIDTH), row),
            pl.BlockSpec((tq, 2 * LANES), row),
            pl.BlockSpec((BLOCK, 2 * LANES), prev),
            pl.BlockSpec((tq, 2 * LANES), row),
            pl.BlockSpec((BLOCK, 2 * LANES), prev),
        ],
        out_specs=pl.BlockSpec((tq, ATTN_WIDTH), row),
        out_shape=jax.ShapeDtypeStruct((t, ATTN_WIDTH), BF16),
        compiler_params=pltpu.CompilerParams(
            dimension_semantics=("arbitrary",), vmem_limit_bytes=VMEM_LIMIT_BYTES),
        name="attn",
    )(sinks, q, kd, kd, vd, vd)


def _outproj_kernel(x_ref, a_ref, c_ref, wa_ref, wc_ref, g_ref, b_ref, o_ref):
    sb = SB_OUT
    nsub = x_ref.shape[0] // sb

    def project(r):
        rows = slice(r * sb, (r + 1) * sb)
        return _dot(a_ref[rows, :], wa_ref[...]) + _dot(c_ref[rows, :], wc_ref[...])

    nxt = project(0)
    for r in range(nsub):
        rows = slice(r * sb, (r + 1) * sb)
        y = nxt
        if r + 1 < nsub:
            nxt = project(r + 1)
        z = DEEPNORM_ALPHA * x_ref[rows, :] + y
        o_ref[rows, :] = _layer_norm(z, g_ref[...], b_ref[...]).astype(o_ref.dtype)


def _outproj(x2, attn, conv, w_out, g, b):
    t = x2.shape[0]
    tm = TM_OUT
    row = lambda i: (i, 0)
    const = lambda i: (0, 0)
    return pl.pallas_call(
        _outproj_kernel,
        grid=(t // tm,),
        in_specs=[
            pl.BlockSpec((tm, D_MODEL), row),
            pl.BlockSpec((tm, ATTN_WIDTH), row),
            pl.BlockSpec((tm, CONV_WIDTH), row),
            pl.BlockSpec((ATTN_WIDTH, D_MODEL), const, pipeline_mode=pl.Buffered(1)),
            pl.BlockSpec((CONV_WIDTH, D_MODEL), lambda i: (1, 0), pipeline_mode=pl.Buffered(1)),
            pl.BlockSpec((1, D_MODEL), const),
            pl.BlockSpec((1, D_MODEL), const),
        ],
        out_specs=pl.BlockSpec((tm, D_MODEL), row),
        out_shape=jax.ShapeDtypeStruct((t, D_MODEL), BF16),
        compiler_params=pltpu.CompilerParams(
            dimension_semantics=("arbitrary",), vmem_limit_bytes=VMEM_LIMIT_BYTES),
        name="outproj",
    )(x2, attn, conv, w_out, w_out, g, b)


def _ffn_kernel(x_ref, wa_ref, wg_ref, cwa_ref, cwg_ref, wd_ref, g_ref, b_ref,
                o_ref, carry_a, carry_g, *, tiles_per_seq, first_width):
    i = pl.program_id(0)
    j = pl.program_id(1)
    tm = x_ref.shape[0]
    sb = SB_FFN
    nsub = tm // sb

    tf = wa_ref.shape[1]

    @pl.when(i % tiles_per_seq == 0)
    def _():
        carry_a[j] = jnp.zeros(carry_a.shape[1:], F32)
        carry_g[j] = jnp.zeros(carry_g.shape[1:], F32)

    def body(first, last):
        width = first_width if first else tf
        cols = slice(0, width)
        wa, wg, wd = wa_ref.at[:, cols], wg_ref.at[:, cols], wd_ref.at[cols, :]
        cwa, cwg = cwa_ref.at[:, cols], cwg_ref.at[:, cols]
        piece = SB_FFN_LAST if last else sb

        def up(r):
            xs = x_ref[r * sb:(r + 1) * sb, :]
            return _dot(xs, wa[...]), _dot(xs, wg[...])

        halo_a, halo_g = carry_a[j][:, :width], carry_g[j][:, :width]
        nxt = up(0)
        for r in range(nsub):
            ua, ug = nxt
            if r + 1 < nsub:
                nxt = up(r + 1)
            a = _causal_conv3(ua, halo_a, cwa)
            gate = _causal_conv3(ug, halo_g, cwg)
            halo_a, halo_g = ua[sb - SUBLANES:, :], ug[sb - SUBLANES:, :]
            h = a * jax.nn.sigmoid(a) * gate
            for p in range(sb // piece):
                rows = slice(r * sb + p * piece, r * sb + (p + 1) * piece)
                if first:
                    acc = DEEPNORM_ALPHA * x_ref[rows, :].astype(F32)
                else:
                    acc = o_ref[rows, :]
                acc = acc + _dot(h[p * piece:(p + 1) * piece, :], wd[...])
                if last:
                    acc = _layer_norm(acc, g_ref[...], b_ref[...])
                o_ref[rows, :] = acc
        carry_a[j, :, :width] = halo_a
        carry_g[j, :, :width] = halo_g

    last_j = pl.num_programs(1) - 1
    pl.when(j == 0)(functools.partial(body, True, False))
    pl.when((j > 0) & (j < last_j))(functools.partial(body, False, False))
    pl.when(j == last_j)(functools.partial(body, False, True))


def _ffn(x1, w_up, conv_w, w_down, g, b, seq):
    t = x1.shape[0]
    tm, tf = TM_FFN, TF_FFN
    nf = pl.cdiv(D_FF, tf)
    assert nf >= 2
    first_width = D_FF - (nf - 1) * tf
    const = lambda i, j: (0, 0)
    assert tf % first_width == 0 and D_FF % first_width == 0 and first_width % LANES == 0
    ratio = tf // first_width
    start = lambda j: jnp.maximum(j * ratio - (ratio - 1), 0) * first_width
    start_g = lambda j: (D_FF // first_width + jnp.maximum(j * ratio - (ratio - 1), 0)) * first_width
    el = pl.Element
    return pl.pallas_call(
        functools.partial(_ffn_kernel, tiles_per_seq=seq // tm, first_width=first_width),
        grid=(t // tm, nf),
        in_specs=[
            pl.BlockSpec((tm, D_MODEL), lambda i, j: (i, 0)),
            pl.BlockSpec((el(D_MODEL), el(tf)), lambda i, j: (0, start(j))),
            pl.BlockSpec((el(D_MODEL), el(tf)), lambda i, j: (0, start_g(j))),
            pl.BlockSpec((el(CONV_K), el(tf)), lambda i, j: (0, start(j))),
            pl.BlockSpec((el(CONV_K), el(tf)), lambda i, j: (0, start_g(j))),
            pl.BlockSpec((el(tf), el(D_MODEL)), lambda i, j: (start(j), 0)),
            pl.BlockSpec((1, D_MODEL), const),
            pl.BlockSpec((1, D_MODEL), const),
        ],
        out_specs=pl.BlockSpec((tm, D_MODEL), lambda i, j: (i, 0)),
        out_shape=jax.ShapeDtypeStruct((t, D_MODEL), F32),
        scratch_shapes=[pltpu.VMEM((nf, SUBLANES, tf), F32),
                        pltpu.VMEM((nf, SUBLANES, tf), F32)],
        compiler_params=pltpu.CompilerParams(
            dimension_semantics=("arbitrary", "arbitrary"), vmem_limit_bytes=VMEM_LIMIT_BYTES),
        name="ffn",
    )(x1, w_up, w_up, conv_w, conv_w, w_down, g, b)


def _rope_tables(seq):
    half = HEAD_DIM // 2
    inv_freq = ROPE_THETA ** (-jnp.arange(half, dtype=F32) / half)
    ang = jnp.arange(seq).astype(F32)[:, None] * inv_freq[None, :]
    cos = jnp.cos(ang)
    sin = jnp.sin(ang)
    reps = LANES // HEAD_DIM
    return (jnp.tile(jnp.concatenate([cos, cos], axis=-1), (1, reps)),
            jnp.tile(jnp.concatenate([-sin, sin], axis=-1), (1, reps)))


def kernel(x, w_in, attn_sinks, short_conv_w, w_out, ln1_g, ln1_b,
           ffn_w_up, ffn_conv_w, ffn_w_down, ln2_g, ln2_b):
    b, s, d = x.shape
    assert d == D_MODEL and s % max(TM_IN, TQ_ATTN, TM_OUT, TM_FFN) == 0
    cos_t, sin_t = _rope_tables(s)
    h = x.reshape(b * s, d)
    for l in range(w_in.shape[0]):
        q, kd, vd, conv, w_up16, w_down16 = _inproj(
            h, w_in[l].astype(BF16), cos_t, sin_t, short_conv_w[l], ffn_w_up[l], ffn_w_down[l], s)
        attn = _attn(attn_sinks[l], q, kd, vd, s)
        x1 = _outproj(h, attn, conv, w_out[l], ln1_g[l][None], ln1_b[l][None])
        h = _ffn(x1, w_up16, ffn_conv_w[l], w_down16, ln2_g[l][None], ln2_b[l][None], s)
    return h.reshape(b, s, d)
```
